```python
import jax, jax.numpy as jnp
from jax import lax
import numpy as np

D_MODEL = 1024
BATCH = 16
SEQ = 2048
DEPTH = 4

N_MIXERS = 2
D_FF = 2816
CONV_WIDTH = 31
DN_HEADS = 8
DN_HEAD_DIM = 128
DN_WIDTH = DN_HEADS * DN_HEAD_DIM
SHORT_CONV = 4
CHUNK = 64
N_CONV_LAYERS = (DEPTH + 1) // 2
N_DN_LAYERS = DEPTH // 2
N_SUB = 3
EPS = 1e-6
FFN_RES_WEIGHT = 0.5

kernel_name = "hybrid_conformer_gated_deltanet_block"


def rms_norm(x, g):
    xf = x.astype(jnp.float32)
    y = xf * lax.rsqrt(jnp.mean(xf * xf, axis=-1, keepdims=True) + EPS)
    return (y * g.astype(jnp.float32)).astype(x.dtype)


def layer_norm(x, g, b):
    xf = x.astype(jnp.float32)
    mu = jnp.mean(xf, axis=-1, keepdims=True)
    xc = xf - mu
    var = jnp.mean(xc * xc, axis=-1, keepdims=True)
    y = xc * lax.rsqrt(var + EPS) * g.astype(jnp.float32) + b.astype(jnp.float32)
    return y.astype(x.dtype)


def causal_dwconv(x, w):
    K, C = w.shape
    return lax.conv_general_dilated(
        x, w[:, None, :].astype(x.dtype), window_strides=(1,), padding=[(K - 1, 0)],
        dimension_numbers=("NWC", "WIO", "NWC"), feature_group_count=C)


def swiglu_ffn(h, w_in, w_out):
    gate, up = jnp.split(h @ w_in, 2, axis=-1)
    return (jax.nn.silu(gate) * up) @ w_out


def conv_module(h, w_glu, b_glu, w_dw, b_dw, ln_g, ln_b, w_pw, b_pw):
    a, b = jnp.split(h @ w_glu + b_glu, 2, axis=-1)
    u = a * jax.nn.sigmoid(b)
    u = causal_dwconv(u, w_dw) + b_dw
    u = jax.nn.silu(layer_norm(u, ln_g, ln_b))
    return u @ w_pw + b_pw


def l2norm(t):
    return t * lax.rsqrt(jnp.sum(t * t, axis=-1, keepdims=True) + EPS)


def chunk_gated_delta_rule(q, k, v, g, beta):
    f32 = jnp.float32
    B, T, H, Dh = q.shape
    N = T // CHUNK
    q = l2norm(q.astype(f32)) * (Dh ** -0.5)
    k = l2norm(k.astype(f32))
    v = v.astype(f32)

    def to_chunks(t):
        return t.reshape(B, N, CHUNK, H, -1).transpose(1, 0, 3, 2, 4)

    def to_chunks_s(t):
        return t.reshape(B, N, CHUNK, H).transpose(1, 0, 3, 2)

    q, k, v = to_chunks(q), to_chunks(k), to_chunks(v)
    beta = to_chunks_s(beta.astype(f32))
    g = jnp.cumsum(to_chunks_s(g.astype(f32)), axis=-1)

    causal = jnp.tril(jnp.ones((CHUNK, CHUNK), dtype=bool))
    strict = jnp.tril(jnp.ones((CHUNK, CHUNK), dtype=bool), -1)
    decay = jnp.exp(jnp.where(causal, g[..., :, None] - g[..., None, :], -jnp.inf))

    kb = k * beta[..., None]
    vb = v * beta[..., None]
    A = jnp.where(strict, jnp.einsum("nbhid,nbhjd->nbhij", kb, k) * decay, 0.0)
    eye = jnp.broadcast_to(jnp.eye(CHUNK, dtype=f32), A.shape)
    Tm = lax.linalg.triangular_solve(A, eye, left_side=True, lower=True, unit_diagonal=True)

    u = jnp.einsum("nbhij,nbhjd->nbhid", Tm, vb)
    w = jnp.einsum("nbhij,nbhjd->nbhid", Tm, kb * jnp.exp(g)[..., None])
    qg = q * jnp.exp(g)[..., None]
    intra = jnp.einsum("nbhid,nbhjd->nbhij", q, k) * decay
    g_last = g[..., -1]
    kd = k * jnp.exp(g_last[..., None] - g)[..., None]

    def step(S, inp):
        qg_i, w_i, u_i, intra_i, kd_i, gl_i = inp
        v_new = u_i - jnp.einsum("bhcd,bhde->bhce", w_i, S)
        o_i = jnp.einsum("bhcd,bhde->bhce", qg_i, S) + jnp.einsum("bhij,bhje->bhie", intra_i, v_new)
        S = S * jnp.exp(gl_i)[..., None, None] + jnp.einsum("bhcd,bhce->bhde", kd_i, v_new)
        return S, o_i

    S0 = jnp.zeros((B, H, Dh, v.shape[-1]), dtype=f32)
    _, o = lax.scan(step, S0, (qg, w, u, intra, kd, g_last))
    return o.transpose(1, 0, 3, 2, 4).reshape(B, T, H, -1)


def gated_deltanet(h, w_in, w_sconv, a_log, dt_bias, o_g, w_out):
    B, T, _ = h.shape
    W, H = DN_WIDTH, DN_HEADS
    proj = h @ w_in
    qkv = proj[..., :3 * W]
    z = proj[..., 3 * W:4 * W]
    a = proj[..., 4 * W:4 * W + H]
    b = proj[..., 4 * W + H:]
    qkv = jax.nn.silu(causal_dwconv(qkv, w_sconv))
    q, k, v = [t.reshape(B, T, H, DN_HEAD_DIM) for t in jnp.split(qkv, 3, axis=-1)]
    g = -jnp.exp(a_log.astype(jnp.float32)) * jax.nn.softplus(a.astype(jnp.float32) + dt_bias.astype(jnp.float32))
    beta = jax.nn.sigmoid(b.astype(jnp.float32))
    o = chunk_gated_delta_rule(q, k, v, g, beta).astype(h.dtype)
    o = rms_norm(o, o_g) * jax.nn.silu(z.reshape(B, T, H, DN_HEAD_DIM))
    return o.reshape(B, T, W) @ w_out


def setup_inputs(seed: int = 0) -> dict:
    key = jax.random.key(seed)
    ks = iter(jax.random.split(key, 32))
    f32 = jnp.float32
    D, F, W, H = D_MODEL, D_FF, DN_WIDTH, DN_HEADS
    NA, NB = N_CONV_LAYERS, N_DN_LAYERS

    def nrm(shape, fan_in, mult=1.0):
        return jax.random.normal(next(ks), shape, f32) * (mult * fan_in ** -0.5)

    def small(shape, s=0.02):
        return jax.random.normal(next(ks), shape, f32) * s

    x = jax.random.normal(next(ks), (BATCH, SEQ, D), f32)
    c = jax.random.normal(next(ks), (BATCH, D), f32)
    norm_g = 1.0 + small((DEPTH, N_SUB, D))
    w_ada = nrm((DEPTH, D, N_SUB * 3 * D), D, 0.2)
    b_ada = small((DEPTH, N_SUB * 3 * D))
    w_ffn_in = nrm((DEPTH, 2, D, 2 * F), D)
    w_ffn_out = nrm((DEPTH, 2, F, D), F)
    cm_w_glu = nrm((NA, D, 2 * D), D)
    cm_b_glu = small((NA, 2 * D))
    cm_w_dw = nrm((NA, CONV_WIDTH, D), CONV_WIDTH)
    cm_b_dw = small((NA, D))
    cm_ln_g = 1.0 + small((NA, D))
    cm_ln_b = small((NA, D))
    cm_w_pw = nrm((NA, D, D), D)
    cm_b_pw = small((NA, D))
    dn_w_in = nrm((NB, D, 4 * W + 2 * H), D)
    dn_w_sconv = nrm((NB, SHORT_CONV, 3 * W), SHORT_CONV)
    dn_a_log = jnp.log(jax.random.uniform(next(ks), (NB, H), f32, 1.0, 16.0))
    dt = jnp.exp(jax.random.uniform(next(ks), (NB, H), f32, np.log(1e-3), np.log(1e-1)))
    dn_dt_bias = dt + jnp.log(-jnp.expm1(-dt))
    dn_o_g = 1.0 + small((NB, DN_HEAD_DIM))
    dn_w_out = nrm((NB, W, D), W)
    final_g = 1.0 + small((D,))
    return {"x": x, "c": c, "norm_g": norm_g, "w_ada": w_ada, "b_ada": b_ada,
            "w_ffn_in": w_ffn_in, "w_ffn_out": w_ffn_out,
            "cm_w_glu": cm_w_glu, "cm_b_glu": cm_b_glu, "cm_w_dw": cm_w_dw, "cm_b_dw": cm_b_dw,
            "cm_ln_g": cm_ln_g, "cm_ln_b": cm_ln_b, "cm_w_pw": cm_w_pw, "cm_b_pw": cm_b_pw,
            "dn_w_in": dn_w_in, "dn_w_sconv": dn_w_sconv, "dn_a_log": dn_a_log,
            "dn_dt_bias": dn_dt_bias, "dn_o_g": dn_o_g, "dn_w_out": dn_w_out,
            "final_g": final_g}


def reference(x, c, norm_g, w_ada, b_ada, w_ffn_in, w_ffn_out,
              cm_w_glu, cm_b_glu, cm_w_dw, cm_b_dw, cm_ln_g, cm_ln_b, cm_w_pw, cm_b_pw,
              dn_w_in, dn_w_sconv, dn_a_log, dn_dt_bias, dn_o_g, dn_w_out, final_g):
    B = x.shape[0]
    D = x.shape[-1]
    cs = jax.nn.silu(c)
    for i in range(DEPTH):
        mod = (cs @ w_ada[i] + b_ada[i]).reshape(B, N_SUB, 3, D)

        def modulated(x, j):
            shift, scale = mod[:, j, 0][:, None, :], mod[:, j, 1][:, None, :]
            return rms_norm(x, norm_g[i, j]) * (1.0 + scale) + shift

        def gate(j):
            return 1.0 + mod[:, j, 2][:, None, :]

        h = modulated(x, 0)
        x = x + FFN_RES_WEIGHT * gate(0) * swiglu_ffn(h, w_ffn_in[i, 0], w_ffn_out[i, 0])

        h = modulated(x, 1)
        if i % N_MIXERS == 0:
            a = i // N_MIXERS
            y = conv_module(h, cm_w_glu[a], cm_b_glu[a], cm_w_dw[a], cm_b_dw[a],
                            cm_ln_g[a], cm_ln_b[a], cm_w_pw[a], cm_b_pw[a])
        else:
            m = i // N_MIXERS
            y = gated_deltanet(h, dn_w_in[m], dn_w_sconv[m], dn_a_log[m], dn_dt_bias[m],
                               dn_o_g[m], dn_w_out[m])
        x = x + gate(1) * y

        h = modulated(x, 2)
        x = x + FFN_RES_WEIGHT * gate(2) * swiglu_ffn(h, w_ffn_in[i, 1], w_ffn_out[i, 1])
    return rms_norm(x, final_g)
```

```python
import functools

import jax
import jax.numpy as jnp
from jax import lax
from jax.experimental import pallas as pl
from jax.experimental.pallas import tpu as pltpu

D_MODEL = 1024
DEPTH = 4
N_SUB = 3
D_FF = 2816
CONV_WIDTH = 31
DN_HEADS = 8
DN_HEAD_DIM = 128
DN_WIDTH = DN_HEADS * DN_HEAD_DIM
SHORT_CONV = 4
CHUNK = 64
EPS = 1e-6
FFN_RES_WEIGHT = 0.5

LANES = 128
SUBLANES = 8
VMEM_LIMIT_BYTES = 56 * 1024 * 1024

FFN_TM = 512
FFN_TF = 256
CONV_TM = 256
CONV_HALO = 32
CONV_RB = 32
CONV_CB = 256
DNP_TM = 256
DNP_HALO = 8
DNC_CPS = 2
DNO_TM = 512
MOD_TN = 1024

F32 = jnp.float32
BF16 = jnp.bfloat16


def _params(*sem):
    return pltpu.CompilerParams(dimension_semantics=sem, vmem_limit_bytes=VMEM_LIMIT_BYTES)


def _const_spec(shape):
    nd = len(shape)
    return pl.BlockSpec(shape, lambda *_: (0,) * nd, pipeline_mode=pl.Buffered(1))


def _bdot(a, b):
    return jnp.dot(a.astype(BF16), b.astype(BF16), preferred_element_type=F32)


def _silu(x):
    return x * jax.nn.sigmoid(x)


def _modulated(x, g_ref, mod_ref):
    ms = jnp.mean(x * x, axis=-1, keepdims=True)
    y = x * lax.rsqrt(ms + EPS) * g_ref[...]
    return y * (1.0 + mod_ref[1:2, :]) + mod_ref[0:1, :]


def _mod_kernel(c_ref, w_ref, b_ref, o_ref):
    cs = _silu(c_ref[...])
    o_ref[...] = _bdot(cs, w_ref[...]) + b_ref[...]


def _ada_mod(c, w_ada, b_ada):
    B = c.shape[0]
    n = w_ada.shape[-1]
    return pl.pallas_call(
        _mod_kernel,
        grid=(DEPTH, n // MOD_TN),
        in_specs=[
            pl.BlockSpec((B, D_MODEL), lambda i, j: (0, 0)),
            pl.BlockSpec((None, D_MODEL, MOD_TN), lambda i, j: (i, 0, j)),
            pl.BlockSpec((None, 1, MOD_TN), lambda i, j: (i, 0, j)),
        ],
        out_specs=pl.BlockSpec((None, B, MOD_TN), lambda i, j: (i, 0, j)),
        out_shape=jax.ShapeDtypeStruct((DEPTH, B, n), F32),
        compiler_params=_params("parallel", "parallel"),
        name="ada_mod",
    )(c, w_ada, b_ada.reshape(DEPTH, 1, n))


def _ffn_kernel(x_ref, mod_ref, g_ref, win_ref, wout_ref, fg_ref, o_ref, a_ref, *, final_norm):
    x = x_ref[...]
    hb = _modulated(x, g_ref, mod_ref).astype(BF16)
    for c in range(D_FF // FFN_TF):
        lo = c * FFN_TF
        gate = jnp.dot(hb, win_ref[:, lo:lo + FFN_TF], preferred_element_type=F32)
        up = jnp.dot(hb, win_ref[:, D_FF + lo:D_FF + lo + FFN_TF], preferred_element_type=F32)
        a_ref[:, lo:lo + FFN_TF] = (_silu(gate) * up).astype(BF16)
    y = jnp.dot(a_ref[...], wout_ref[...], preferred_element_type=F32)
    out = x + (FFN_RES_WEIGHT * (1.0 + mod_ref[2:3, :])) * y
    if final_norm:
        ms = jnp.mean(out * out, axis=-1, keepdims=True)
        out = out * lax.rsqrt(ms + EPS) * fg_ref[...]
    o_ref[...] = out


def _ffn(x, mod, sub, norm_g, w_in, w_out, final_g, final_norm):
    B, T, D = x.shape
    return pl.pallas_call(
        functools.partial(_ffn_kernel, final_norm=final_norm),
        grid=(B, T // FFN_TM),
        in_specs=[
            pl.BlockSpec((None, FFN_TM, D), lambda b, t: (b, t, 0)),
            pl.BlockSpec((None, None, 3, D), lambda b, t: (b, sub, 0, 0)),
            _const_spec((1, D)),
            _const_spec((D, 2 * D_FF)),
            _const_spec((D_FF, D)),
            _const_spec((1, D)),
        ],
        out_specs=pl.BlockSpec((None, FFN_TM, D), lambda b, t: (b, t, 0)),
        out_shape=jax.ShapeDtypeStruct((B, T, D), F32),
        scratch_shapes=[pltpu.VMEM((FFN_TM, D_FF), BF16)],
        compiler_params=_params("parallel", "parallel"),
        name="ffn",
    )(x, mod, norm_g.reshape(1, D), w_in, w_out, final_g.reshape(1, D))


def _conv_kernel(x_ref, mod_ref, g_ref, wglu_ref, bglu_ref, wdw_ref, bdw_ref, lng_ref, lnb_ref,
                 wpw_ref, bpw_ref, o_ref, u_ref, c_ref):
    D = D_MODEL
    tm = CONV_TM

    @pl.when(pl.program_id(1) == 0)
    def _():
        u_ref[0:CONV_HALO, :] = jnp.zeros((CONV_HALO, D), F32)

    x = x_ref[...]
    hb = _modulated(x, g_ref, mod_ref).astype(BF16)
    a = jnp.dot(hb, wglu_ref[:, 0:D], preferred_element_type=F32) + bglu_ref[:, 0:D]
    b = jnp.dot(hb, wglu_ref[:, D:2 * D], preferred_element_type=F32) + bglu_ref[:, D:2 * D]
    u_ref[CONV_HALO:CONV_HALO + tm, :] = a * jax.nn.sigmoid(b)

    base = CONV_HALO - (CONV_WIDTH - 1)
    for rb in range(tm // CONV_RB):
        r0 = rb * CONV_RB
        for cb in range(D // CONV_CB):
            c0 = cb * CONV_CB
            acc = jnp.zeros((CONV_RB, CONV_CB), F32)
            for k in range(CONV_WIDTH):
                acc = acc + wdw_ref[k:k + 1, c0:c0 + CONV_CB] * u_ref[base + r0 + k:base + r0 + k + CONV_RB, c0:c0 + CONV_CB]
            c_ref[r0:r0 + CONV_RB, c0:c0 + CONV_CB] = acc
    u_ref[0:CONV_HALO, :] = u_ref[tm:tm + CONV_HALO, :]

    cv = c_ref[...] + bdw_ref[...]
    mu = jnp.mean(cv, axis=-1, keepdims=True)
    xc = cv - mu
    var = jnp.mean(xc * xc, axis=-1, keepdims=True)
    y = xc * lax.rsqrt(var + EPS) * lng_ref[...] + lnb_ref[...]
    out = _bdot(_silu(y), wpw_ref[...]) + bpw_ref[...]
    o_ref[...] = x + (1.0 + mod_ref[2:3, :]) * out


def _conv_module(x, mod, norm_g, w_glu, b_glu, w_dw, b_dw, ln_g, ln_b, w_pw, b_pw):
    B, T, D = x.shape
    return pl.pallas_call(
        _conv_kernel,
        grid=(B, T // CONV_TM),
        in_specs=[
            pl.BlockSpec((None, CONV_TM, D), lambda b, t: (b, t, 0)),
            pl.BlockSpec((None, None, 3, D), lambda b, t: (b, 1, 0, 0)),
            _const_spec((1, D)),
            _const_spec((D, 2 * D)),
            _const_spec((1, 2 * D)),
            _const_spec((CONV_WIDTH, D)),
            _const_spec((1, D)),
            _const_spec((1, D)),
            _const_spec((1, D)),
            _const_spec((D, D)),
            _const_spec((1, D)),
        ],
        out_specs=pl.BlockSpec((None, CONV_TM, D), lambda b, t: (b, t, 0)),
        out_shape=jax.ShapeDtypeStruct((B, T, D), F32),
        scratch_shapes=[pltpu.VMEM((CONV_HALO + CONV_TM, D), F32), pltpu.VMEM((CONV_TM, D), F32)],
        compiler_params=_params("arbitrary", "arbitrary"),
        name="conv_module",
    )(x, mod, norm_g.reshape(1, D), w_glu, b_glu.reshape(1, 2 * D), w_dw, b_dw.reshape(1, D),
      ln_g.reshape(1, D), ln_b.reshape(1, D), w_pw, b_pw.reshape(1, D))


def _dnproj_kernel(x_ref, mod_ref, g_ref, wqkvz_ref, wab_ref, wsc_ref, alog_ref, dtb_ref,
                   q_ref, k_ref, v_ref, z_ref, gcol_ref, grow_ref, pre_ref):
    W = DN_WIDTH
    tm = DNP_TM

    @pl.when(pl.program_id(1) == 0)
    def _():
        pre_ref[0:DNP_HALO, :] = jnp.zeros((DNP_HALO, 3 * W), F32)

    hb = _modulated(x_ref[...], g_ref, mod_ref).astype(BF16)
    z_ref[...] = jnp.dot(hb, wqkvz_ref[:, 3 * W:4 * W], preferred_element_type=F32).astype(BF16)

    ab = jnp.dot(hb, wab_ref[...], preferred_element_type=F32)
    lane = lax.broadcasted_iota(jnp.int32, (tm, LANES), 1)
    g = -jnp.exp(alog_ref[...]) * jax.nn.softplus(ab + dtb_ref[...])
    g = jnp.where(lane < DN_HEADS, g, 0.0)
    ri = lax.broadcasted_iota(jnp.int32, (tm, tm), 0)
    ci = lax.broadcasted_iota(jnp.int32, (tm, tm), 1)
    tri = jnp.where((ri // CHUNK == ci // CHUNK) & (ci <= ri), 1.0, 0.0).astype(F32)
    gc = jnp.dot(tri, g, preferred_element_type=F32, precision=lax.Precision.HIGHEST)
    col = jnp.where(lane < DN_HEADS, gc, jax.nn.sigmoid(ab))
    gcol_ref[...] = col
    colt = col.T
    for c in range(tm // CHUNK):
        grow_ref[c] = colt[0:2 * DN_HEADS, c * CHUNK:(c + 1) * CHUNK]

    for j in range(3):
        pre_ref[DNP_HALO:DNP_HALO + tm, j * W:(j + 1) * W] = jnp.dot(
            hb, wqkvz_ref[:, j * W:(j + 1) * W], preferred_element_type=F32)
    base = DNP_HALO - (SHORT_CONV - 1)
    outs = (q_ref, k_ref, v_ref)
    for j in range(3):
        for h in range(DN_HEADS):
            c0 = j * W + h * DN_HEAD_DIM
            acc = jnp.zeros((tm, DN_HEAD_DIM), F32)
            for k in range(SHORT_CONV):
                acc = acc + wsc_ref[k:k + 1, c0:c0 + DN_HEAD_DIM] * pre_ref[base + k:base + k + tm, c0:c0 + DN_HEAD_DIM]
            y = _silu(acc)
            if j < 2:
                y = y * lax.rsqrt(jnp.sum(y * y, axis=-1, keepdims=True) + EPS)
                if j == 0:
                    y = y * (DN_HEAD_DIM ** -0.5)
            outs[j][:, h * DN_HEAD_DIM:(h + 1) * DN_HEAD_DIM] = y.astype(BF16)
    pre_ref[0:DNP_HALO, :] = pre_ref[tm:tm + DNP_HALO, :]


def _dn_proj(x, mod, norm_g, w_qkvz, w_ab, w_sconv, a_log, dt_bias):
    B, T, D = x.shape
    W = DN_WIDTH
    tok = pl.BlockSpec((None, DNP_TM, W), lambda b, t: (b, t, 0))
    cps = DNP_TM // CHUNK
    return pl.pallas_call(
        _dnproj_kernel,
        grid=(B, T // DNP_TM),
        in_specs=[
            pl.BlockSpec((None, DNP_TM, D), lambda b, t: (b, t, 0)),
            pl.BlockSpec((None, None, 3, D), lambda b, t: (b, 1, 0, 0)),
            _const_spec((1, D)),
            _const_spec((D, 4 * W)),
            _const_spec((D, LANES)),
            _const_spec((SHORT_CONV, 3 * W)),
            _const_spec((1, LANES)),
            _const_spec((1, LANES)),
        ],
        out_specs=[tok, tok, tok, tok,
                   pl.BlockSpec((None, DNP_TM, LANES), lambda b, t: (b, t, 0)),
                   pl.BlockSpec((None, cps, 2 * DN_HEADS, CHUNK), lambda b, t: (b, t, 0, 0))],
        out_shape=[jax.ShapeDtypeStruct((B, T, W), BF16)] * 4 + [
            jax.ShapeDtypeStruct((B, T, LANES), F32),
            jax.ShapeDtypeStruct((B, T // CHUNK, 2 * DN_HEADS, CHUNK), F32)],
        scratch_shapes=[pltpu.VMEM((DNP_HALO + DNP_TM, 3 * W), F32)],
        compiler_params=_params("arbitrary", "arbitrary"),
        name="dn_proj",
    )(x, mod, norm_g.reshape(1, D), w_qkvz, w_ab, w_sconv, a_log, dt_bias)


def _dnchunk_kernel(q_ref, k_ref, v_ref, gcol_ref, grow_ref, o_ref, s_ref):
    C = CHUNK
    Dh = DN_HEAD_DIM

    @pl.when(pl.program_id(1) == 0)
    def _():
        s_ref[...] = jnp.zeros(s_ref.shape, F32)

    ri = lax.broadcasted_iota(jnp.int32, (C, C), 0)
    ci = lax.broadcasted_iota(jnp.int32, (C, C), 1)
    causal = ci <= ri
    strict = ci < ri
    eye = jnp.where(ri == ci, 1.0, 0.0).astype(F32)

    def same_block(n):
        return (ri // n) == (ci // n)

    for c in range(DNC_CPS):
        r0 = c * C
        for h in range(DN_HEADS):
            hs = slice(h * Dh, (h + 1) * Dh)
            qb = q_ref[r0:r0 + C, hs]
            kb16 = k_ref[r0:r0 + C, hs]
            kf = kb16.astype(F32)
            vf = v_ref[r0:r0 + C, hs].astype(F32)
            gc = gcol_ref[r0:r0 + C, h:h + 1]
            beta = gcol_ref[r0:r0 + C, DN_HEADS + h:DN_HEADS + h + 1]
            gr = grow_ref[c, h:h + 1, :]
            glast = gc[C - 1:C, :]

            decay = jnp.where(causal, jnp.exp(jnp.where(causal, gc - gr, 0.0)), 0.0)
            kbeta = kf * beta
            kk = lax.dot_general(kbeta.astype(BF16), kb16, (((1,), (1,)), ((), ())), preferred_element_type=F32)
            qk = lax.dot_general(qb, kb16, (((1,), (1,)), ((), ())), preferred_element_type=F32)
            a_mat = jnp.where(strict, kk * decay, 0.0)

            d8 = jnp.where(same_block(8), a_mat, 0.0)
            d8_2 = _bdot(d8, d8)
            d8_4 = _bdot(d8_2, d8_2)
            tinv = _bdot(_bdot(eye - d8, eye + d8_2), eye + d8_4)
            for n in (8, 16, 32):
                off = jnp.where(same_block(2 * n) & jnp.logical_not(same_block(n)), a_mat, 0.0)
                tinv = tinv - _bdot(tinv, _bdot(off, tinv))

            eg = jnp.exp(gc)
            rhs = jnp.concatenate([(vf * beta).astype(BF16), (kbeta * eg).astype(BF16)], axis=1)
            uw = jnp.dot(tinv.astype(BF16), rhs, preferred_element_type=F32)
            u = uw[:, 0:Dh]
            w = uw[:, Dh:2 * Dh]

            s = s_ref[h]
            sb = s.astype(BF16)
            v_new = u - jnp.dot(w.astype(BF16), sb, preferred_element_type=F32)
            vnb = v_new.astype(BF16)
            qg = (qb.astype(F32) * eg).astype(BF16)
            o = jnp.dot(qg, sb, preferred_element_type=F32) + jnp.dot(
                (qk * decay).astype(BF16), vnb, preferred_element_type=F32)
            o_ref[r0:r0 + C, hs] = o.astype(BF16)

            kd = (kf * jnp.exp(glast - gc)).astype(BF16)
            s_ref[h] = s * jnp.exp(glast) + lax.dot_general(
                kd, vnb, (((0,), (0,)), ((), ())), preferred_element_type=F32)


def _dn_chunk(q, k, v, gcol, grow):
    B, T, W = q.shape
    tc = DNC_CPS * CHUNK
    tok = pl.BlockSpec((None, tc, W), lambda b, t: (b, t, 0))
    return pl.pallas_call(
        _dnchunk_kernel,
        grid=(B, T // tc),
        in_specs=[tok, tok, tok,
                  pl.BlockSpec((None, tc, LANES), lambda b, t: (b, t, 0)),
                  pl.BlockSpec((None, DNC_CPS, 2 * DN_HEADS, CHUNK), lambda b, t: (b, t, 0, 0))],
        out_specs=tok,
        out_shape=jax.ShapeDtypeStruct((B, T, W), BF16),
        scratch_shapes=[pltpu.VMEM((DN_HEADS, DN_HEAD_DIM, DN_HEAD_DIM), F32)],
        compiler_params=_params("arbitrary", "arbitrary"),
        name="dn_chunk",
    )(q, k, v, gcol, grow)


def _dnout_kernel(x_ref, mod_ref, o_ref_in, z_ref, og_ref, wout_ref, out_ref, y_ref):
    Dh = DN_HEAD_DIM
    for h in range(DN_HEADS):
        hs = slice(h * Dh, (h + 1) * Dh)
        o = o_ref_in[:, hs].astype(F32)
        ms = jnp.mean(o * o, axis=-1, keepdims=True)
        y = o * lax.rsqrt(ms + EPS) * og_ref[...] * _silu(z_ref[:, hs].astype(F32))
        y_ref[:, hs] = y.astype(BF16)
    out = jnp.dot(y_ref[...], wout_ref[...], preferred_element_type=F32)
    out_ref[...] = x_ref[...] + (1.0 + mod_ref[2:3, :]) * out


def _dn_out(x, mod, o, z, o_g, w_out):
    B, T, D = x.shape
    W = DN_WIDTH
    return pl.pallas_call(
        _dnout_kernel,
        grid=(B, T // DNO_TM),
        in_specs=[
            pl.BlockSpec((None, DNO_TM, D), lambda b, t: (b, t, 0)),
            pl.BlockSpec((None, None, 3, D), lambda b, t: (b, 1, 0, 0)),
            pl.BlockSpec((None, DNO_TM, W), lambda b, t: (b, t, 0)),
            pl.BlockSpec((None, DNO_TM, W), lambda b, t: (b, t, 0)),
            _const_spec((1, DN_HEAD_DIM)),
            _const_spec((W, D)),
        ],
        out_specs=pl.BlockSpec((None, DNO_TM, D), lambda b, t: (b, t, 0)),
        out_shape=jax.ShapeDtypeStruct((B, T, D), F32),
        scratch_shapes=[pltpu.VMEM((DNO_TM, W), BF16)],
        compiler_params=_params("parallel", "parallel"),
        name="dn_out",
    )(x, mod, o, z, o_g.reshape(1, DN_HEAD_DIM), w_out)


def _lane_pad_row(v):
    return jnp.pad(v.astype(F32), (0, LANES - v.shape[0])).reshape(1, LANES)


def kernel(x, c, norm_g, w_ada, b_ada, w_ffn_in, w_ffn_out, cm_w_glu, cm_b_glu, cm_w_dw, cm_b_dw, cm_ln_g, cm_ln_b, cm_w_pw, cm_b_pw, dn_w_in, dn_w_sconv, dn_a_log, dn_dt_bias, dn_o_g, dn_w_out, final_g):
    B = x.shape[0]
    W = DN_WIDTH
    mod_all = _ada_mod(c, w_ada, b_ada).reshape(DEPTH, B, N_SUB, 3, D_MODEL)
    for i in range(DEPTH):
        mod = mod_all[i]
        x = _ffn(x, mod, 0, norm_g[i, 0], w_ffn_in[i, 0].astype(BF16), w_ffn_out[i, 0].astype(BF16),
                 final_g, False)
        if i % 2 == 0:
            a = i // 2
            x = _conv_module(x, mod, norm_g[i, 1], cm_w_glu[a].astype(BF16), cm_b_glu[a], cm_w_dw[a],
                             cm_b_dw[a], cm_ln_g[a], cm_ln_b[a], cm_w_pw[a].astype(BF16), cm_b_pw[a])
        else:
            m = i // 2
            w_in = dn_w_in[m]
            w_ab = jnp.pad(w_in[:, 4 * W:], ((0, 0), (0, LANES - 2 * DN_HEADS))).astype(BF16)
            q, k, v, z, gcol, grow = _dn_proj(
                x, mod, norm_g[i, 1], w_in[:, :4 * W].astype(BF16), w_ab, dn_w_sconv[m],
                _lane_pad_row(dn_a_log[m]), _lane_pad_row(dn_dt_bias[m]))
            o = _dn_chunk(q, k, v, gcol, grow)
            x = _dn_out(x, mod, o, z, dn_o_g[m], dn_w_out[m].astype(BF16))
        x = _ffn(x, mod, 2, norm_g[i, 2], w_ffn_in[i, 1].astype(BF16), w_ffn_out[i, 1].astype(BF16),
                 final_g, i == DEPTH - 1)
    return x
```

```python
import functools

import jax
import jax.numpy as jnp
from jax import lax
from jax.experimental import pallas as pl
from jax.experimental.pallas import tpu as pltpu

D_MODEL = 1024
DEPTH = 4
N_SUB = 3
D_FF = 2816
CONV_WIDTH = 31
DN_HEADS = 8
DN_HEAD_DIM = 128
DN_WIDTH = DN_HEADS * DN_HEAD_DIM
SHORT_CONV = 4
CHUNK = 64
EPS = 1e-6
FFN_RES_WEIGHT = 0.5

LANES = 128
SUBLANES = 8
VMEM_LIMIT_BYTES = 56 * 1024 * 1024

FFN_TM = 512
FFN_TF = 256
CONV_TM = 256
CONV_HALO = 32
CONV_RB = 64
CONV_CB = 128
DNP_TM = 256
DNP_HALO = 8
DNC_CPS = 4
DNO_TM = 512
MOD_TN = 1024

F32 = jnp.float32
BF16 = jnp.bfloat16


def _params(*sem):
    return pltpu.CompilerParams(dimension_semantics=sem, vmem_limit_bytes=VMEM_LIMIT_BYTES)


def _const_spec(shape):
    nd = len(shape)
    return pl.BlockSpec(shape, lambda *_: (0,) * nd, pipeline_mode=pl.Buffered(1))


def _bdot(a, b):
    return jnp.dot(a.astype(BF16), b.astype(BF16), preferred_element_type=F32)


def _silu(x):
    return x * jax.nn.sigmoid(x)


def _modulated(x, g_ref, mod_ref):
    ms = jnp.mean(x * x, axis=-1, keepdims=True)
    y = x * lax.rsqrt(ms + EPS) * g_ref[...]
    return y * (1.0 + mod_ref[1:2, :]) + mod_ref[0:1, :]


def _mod_kernel(c_ref, w_ref, b_ref, o_ref):
    cs = _silu(c_ref[...])
    o_ref[...] = _bdot(cs, w_ref[...]) + b_ref[...]


def _ada_mod(c, w_ada, b_ada):
    B = c.shape[0]
    n = w_ada.shape[-1]
    return pl.pallas_call(
        _mod_kernel,
        grid=(DEPTH, n // MOD_TN),
        in_specs=[
            pl.BlockSpec((B, D_MODEL), lambda i, j: (0, 0)),
            pl.BlockSpec((None, D_MODEL, MOD_TN), lambda i, j: (i, 0, j)),
            pl.BlockSpec((None, 1, MOD_TN), lambda i, j: (i, 0, j)),
        ],
        out_specs=pl.BlockSpec((None, B, MOD_TN), lambda i, j: (i, 0, j)),
        out_shape=jax.ShapeDtypeStruct((DEPTH, B, n), F32),
        compiler_params=_params("parallel", "parallel"),
        name="ada_mod",
    )(c, w_ada, b_ada.reshape(DEPTH, 1, n))


def _ffn_kernel(x_ref, mod_ref, g_ref, win_ref, wout_ref, fg_ref, o_ref, a_ref, *, final_norm):
    x = x_ref[...]
    hb = _modulated(x, g_ref, mod_ref).astype(BF16)
    for c in range(D_FF // FFN_TF):
        lo = c * FFN_TF
        gate = jnp.dot(hb, win_ref[:, lo:lo + FFN_TF], preferred_element_type=F32)
        up = jnp.dot(hb, win_ref[:, D_FF + lo:D_FF + lo + FFN_TF], preferred_element_type=F32)
        a_ref[:, lo:lo + FFN_TF] = (_silu(gate) * up).astype(BF16)
    y = jnp.dot(a_ref[...], wout_ref[...], preferred_element_type=F32)
    out = x + (FFN_RES_WEIGHT * (1.0 + mod_ref[2:3, :])) * y
    if final_norm:
        ms = jnp.mean(out * out, axis=-1, keepdims=True)
        out = out * lax.rsqrt(ms + EPS) * fg_ref[...]
    o_ref[...] = out


def _ffn(x, mod, sub, norm_g, w_in, w_out, final_g, final_norm):
    B, T, D = x.shape
    return pl.pallas_call(
        functools.partial(_ffn_kernel, final_norm=final_norm),
        grid=(B, T // FFN_TM),
        in_specs=[
            pl.BlockSpec((None, FFN_TM, D), lambda b, t: (b, t, 0)),
            pl.BlockSpec((None, None, 3, D), lambda b, t: (b, sub, 0, 0)),
            _const_spec((1, D)),
            _const_spec((D, 2 * D_FF)),
            _const_spec((D_FF, D)),
            _const_spec((1, D)),
        ],
        out_specs=pl.BlockSpec((None, FFN_TM, D), lambda b, t: (b, t, 0)),
        out_shape=jax.ShapeDtypeStruct((B, T, D), F32),
        scratch_shapes=[pltpu.VMEM((FFN_TM, D_FF), BF16)],
        compiler_params=_params("parallel", "parallel"),
        name="ffn",
    )(x, mod, norm_g.reshape(1, D), w_in, w_out, final_g.reshape(1, D))


def _conv_kernel(x_ref, mod_ref, g_ref, wglu_ref, bglu_ref, wdw_ref, bdw_ref, lng_ref, lnb_ref,
                 wpw_ref, bpw_ref, o_ref, u_ref, c_ref):
    D = D_MODEL
    tm = CONV_TM

    @pl.when(pl.program_id(1) == 0)
    def _():
        u_ref[0:CONV_HALO, :] = jnp.zeros((CONV_HALO, D), F32)

    x = x_ref[...]
    hb = _modulated(x, g_ref, mod_ref).astype(BF16)
    a = jnp.dot(hb, wglu_ref[:, 0:D], preferred_element_type=F32) + bglu_ref[:, 0:D]
    b = jnp.dot(hb, wglu_ref[:, D:2 * D], preferred_element_type=F32) + bglu_ref[:, D:2 * D]
    u_ref[CONV_HALO:CONV_HALO + tm, :] = a * jax.nn.sigmoid(b)

    base = CONV_HALO - (CONV_WIDTH - 1)
    for rb in range(tm // CONV_RB):
        r0 = rb * CONV_RB
        for cb in range(D // CONV_CB):
            cs = slice(cb * CONV_CB, (cb + 1) * CONV_CB)
            acc = None
            for s in range(SUBLANES):
                taps = [k for k in range(CONV_WIDTH) if (base + k) % SUBLANES == s]
                rows = (base + taps[-1]) // SUBLANES * SUBLANES + CONV_RB
                slab = u_ref[r0 + s:r0 + s + rows, cs]
                part = None
                for k in taps:
                    j0 = (base + k) // SUBLANES * SUBLANES
                    term = wdw_ref[k:k + 1, cs] * slab[j0:j0 + CONV_RB]
                    part = term if part is None else part + term
                acc = part if acc is None else acc + part
            c_ref[r0:r0 + CONV_RB, cs] = acc
    u_ref[0:CONV_HALO, :] = u_ref[tm:tm + CONV_HALO, :]

    cv = c_ref[...] + bdw_ref[...]
    mu = jnp.mean(cv, axis=-1, keepdims=True)
    xc = cv - mu
    var = jnp.mean(xc * xc, axis=-1, keepdims=True)
    y = xc * lax.rsqrt(var + EPS) * lng_ref[...] + lnb_ref[...]
    out = _bdot(_silu(y), wpw_ref[...]) + bpw_ref[...]
    o_ref[...] = x + (1.0 + mod_ref[2:3, :]) * out


def _conv_module(x, mod, norm_g, w_glu, b_glu, w_dw, b_dw, ln_g, ln_b, w_pw, b_pw):
    B, T, D = x.shape
    return pl.pallas_call(
        _conv_kernel,
        grid=(B, T // CONV_TM),
        in_specs=[
            pl.BlockSpec((None, CONV_TM, D), lambda b, t: (b, t, 0)),
            pl.BlockSpec((None, None, 3, D), lambda b, t: (b, 1, 0, 0)),
            _const_spec((1, D)),
            _const_spec((D, 2 * D)),
            _const_spec((1, 2 * D)),
            _const_spec((CONV_WIDTH, D)),
            _const_spec((1, D)),
            _const_spec((1, D)),
            _const_spec((1, D)),
            _const_spec((D, D)),
            _const_spec((1, D)),
        ],
        out_specs=pl.BlockSpec((None, CONV_TM, D), lambda b, t: (b, t, 0)),
        out_shape=jax.ShapeDtypeStruct((B, T, D), F32),
        scratch_shapes=[pltpu.VMEM((CONV_HALO + CONV_TM, D), F32), pltpu.VMEM((CONV_TM, D), F32)],
        compiler_params=_params("arbitrary", "arbitrary"),
        name="conv_module",
    )(x, mod, norm_g.reshape(1, D), w_glu, b_glu.reshape(1, 2 * D), w_dw, b_dw.reshape(1, D),
      ln_g.reshape(1, D), ln_b.reshape(1, D), w_pw, b_pw.reshape(1, D))


def _dnproj_kernel(x_ref, mod_ref, g_ref, wqkvz_ref, wab_ref, wsc_ref, alog_ref, dtb_ref,
                   q_ref, k_ref, v_ref, z_ref, gcol_ref, grow_ref, pre_ref):
    W = DN_WIDTH
    tm = DNP_TM

    @pl.when(pl.program_id(1) == 0)
    def _():
        pre_ref[0:DNP_HALO, :] = jnp.zeros((DNP_HALO, 3 * W), F32)

    hb = _modulated(x_ref[...], g_ref, mod_ref).astype(BF16)
    z_ref[...] = jnp.dot(hb, wqkvz_ref[:, 3 * W:4 * W], preferred_element_type=F32).astype(BF16)

    ab = jnp.dot(hb, wab_ref[...], preferred_element_type=F32)
    lane = lax.broadcasted_iota(jnp.int32, (tm, LANES), 1)
    g = -jnp.exp(alog_ref[...]) * jax.nn.softplus(ab + dtb_ref[...])
    g = jnp.where(lane < DN_HEADS, g, 0.0)
    ri = lax.broadcasted_iota(jnp.int32, (tm, tm), 0)
    ci = lax.broadcasted_iota(jnp.int32, (tm, tm), 1)
    tri = jnp.where((ri // CHUNK == ci // CHUNK) & (ci <= ri), 1.0, 0.0).astype(F32)
    gc = jnp.dot(tri, g, preferred_element_type=F32, precision=lax.Precision.HIGHEST)
    col = jnp.where(lane < DN_HEADS, gc, jax.nn.sigmoid(ab))
    gcol_ref[...] = col
    colt = col.T
    for c in range(tm // CHUNK):
        grow_ref[c] = colt[0:2 * DN_HEADS, c * CHUNK:(c + 1) * CHUNK]

    for j in range(3):
        pre_ref[DNP_HALO:DNP_HALO + tm, j * W:(j + 1) * W] = jnp.dot(
            hb, wqkvz_ref[:, j * W:(j + 1) * W], preferred_element_type=F32)
    base = DNP_HALO - (SHORT_CONV - 1)
    outs = (q_ref, k_ref, v_ref)
    for j in range(3):
        for h in range(DN_HEADS):
            c0 = j * W + h * DN_HEAD_DIM
            acc = jnp.zeros((tm, DN_HEAD_DIM), F32)
            for k in range(SHORT_CONV):
                acc = acc + wsc_ref[k:k + 1, c0:c0 + DN_HEAD_DIM] * pre_ref[base + k:base + k + tm, c0:c0 + DN_HEAD_DIM]
            y = _silu(acc)
            if j < 2:
                y = y * lax.rsqrt(jnp.sum(y * y, axis=-1, keepdims=True) + EPS)
                if j == 0:
                    y = y * (DN_HEAD_DIM ** -0.5)
            outs[j][:, h * DN_HEAD_DIM:(h + 1) * DN_HEAD_DIM] = y.astype(BF16)
    pre_ref[0:DNP_HALO, :] = pre_ref[tm:tm + DNP_HALO, :]


def _dn_proj(x, mod, norm_g, w_qkvz, w_ab, w_sconv, a_log, dt_bias):
    B, T, D = x.shape
    W = DN_WIDTH
    tok = pl.BlockSpec((None, DNP_TM, W), lambda b, t: (b, t, 0))
    cps = DNP_TM // CHUNK
    return pl.pallas_call(
        _dnproj_kernel,
        grid=(B, T // DNP_TM),
        in_specs=[
            pl.BlockSpec((None, DNP_TM, D), lambda b, t: (b, t, 0)),
            pl.BlockSpec((None, None, 3, D), lambda b, t: (b, 1, 0, 0)),
            _const_spec((1, D)),
            _const_spec((D, 4 * W)),
            _const_spec((D, LANES)),
            _const_spec((SHORT_CONV, 3 * W)),
            _const_spec((1, LANES)),
            _const_spec((1, LANES)),
        ],
        out_specs=[tok, tok, tok, tok,
                   pl.BlockSpec((None, DNP_TM, LANES), lambda b, t: (b, t, 0)),
                   pl.BlockSpec((None, cps, 2 * DN_HEADS, CHUNK), lambda b, t: (b, t, 0, 0))],
        out_shape=[jax.ShapeDtypeStruct((B, T, W), BF16)] * 4 + [
            jax.ShapeDtypeStruct((B, T, LANES), F32),
            jax.ShapeDtypeStruct((B, T // CHUNK, 2 * DN_HEADS, CHUNK), F32)],
        scratch_shapes=[pltpu.VMEM((DNP_HALO + DNP_TM, 3 * W), F32)],
        compiler_params=_params("arbitrary", "arbitrary"),
        name="dn_proj",
    )(x, mod, norm_g.reshape(1, D), w_qkvz, w_ab, w_sconv, a_log, dt_bias)


def _dnchunk_kernel(q_ref, k_ref, v_ref, gcol_ref, grow_ref, o_ref, s_ref):
    C = CHUNK
    Dh = DN_HEAD_DIM
    H = DN_HEADS
    NT = (((1,), (1,)), ((), ()))
    TN = (((0,), (0,)), ((), ()))

    @pl.when(pl.program_id(1) == 0)
    def _():
        s_ref[...] = jnp.zeros(s_ref.shape, F32)

    ri = lax.broadcasted_iota(jnp.int32, (C, C), 0)
    ci = lax.broadcasted_iota(jnp.int32, (C, C), 1)
    causal = ci <= ri
    strict = ci < ri
    eye = jnp.where(ri == ci, 1.0, 0.0).astype(F32)
    same = {m: lax.shift_right_logical(ri, s) == lax.shift_right_logical(ci, s)
            for m, s in ((8, 3), (16, 4), (32, 5))}
    off = {8: same[16] & jnp.logical_not(same[8]),
           16: same[32] & jnp.logical_not(same[16]),
           32: jnp.logical_not(same[32])}

    items = [(c, h) for c in range(DNC_CPS) for h in range(H)]
    n = len(items)

    k16, q16, kf, beta, gc, eg, decay, kbeta = ([None] * n for _ in range(8))
    for i, (c, h) in enumerate(items):
        r0 = c * C
        hs = slice(h * Dh, (h + 1) * Dh)
        k16[i] = k_ref[r0:r0 + C, hs]
        q16[i] = q_ref[r0:r0 + C, hs]
        kf[i] = k16[i].astype(F32)
        gc[i] = gcol_ref[r0:r0 + C, h:h + 1]
        beta[i] = gcol_ref[r0:r0 + C, H + h:H + h + 1]
        gr = grow_ref[c, h:h + 1, :]
        decay[i] = jnp.where(causal, jnp.exp(jnp.where(causal, gc[i] - gr, 0.0)), 0.0)
        kbeta[i] = kf[i] * beta[i]
        eg[i] = jnp.exp(gc[i])

    kq = [lax.dot_general(jnp.concatenate([kbeta[i].astype(BF16), q16[i]], axis=0), k16[i], NT,
                          preferred_element_type=F32) for i in range(n)]
    a_mat = [jnp.where(strict, kq[i][0:C] * decay[i], 0.0) for i in range(n)]
    intra = [(kq[i][C:2 * C] * decay[i]).astype(BF16) for i in range(n)]

    d8 = [jnp.where(same[8], a_mat[i], 0.0) for i in range(n)]
    d8_2 = [_bdot(d8[i], d8[i]) for i in range(n)]
    d8_4 = [_bdot(d8_2[i], d8_2[i]) for i in range(n)]
    p8 = [_bdot(eye - d8[i], eye + d8_2[i]) for i in range(n)]
    tinv = [_bdot(p8[i], eye + d8_4[i]) for i in range(n)]
    for m in (8, 16, 32):
        et = [_bdot(jnp.where(off[m], a_mat[i], 0.0), tinv[i]) for i in range(n)]
        tinv = [tinv[i] - _bdot(tinv[i], et[i]) for i in range(n)]

    uw = []
    for i, (c, h) in enumerate(items):
        r0 = c * C
        vf = v_ref[r0:r0 + C, h * Dh:(h + 1) * Dh].astype(F32)
        rhs = jnp.concatenate([(vf * beta[i]).astype(BF16), (kbeta[i] * eg[i]).astype(BF16)], axis=1)
        uw.append(jnp.dot(tinv[i].astype(BF16), rhs, preferred_element_type=F32))

    state = [s_ref[h] for h in range(H)]
    for c in range(DNC_CPS):
        r0 = c * C
        idx = [c * H + h for h in range(H)]
        sb = [state[h].astype(BF16) for h in range(H)]
        wq = [jnp.concatenate([uw[i][:, Dh:2 * Dh].astype(BF16),
                               (q16[i].astype(F32) * eg[i]).astype(BF16)], axis=0) for i in idx]
        ws = [jnp.dot(wq[h], sb[h], preferred_element_type=F32) for h in range(H)]
        vnb = [(uw[idx[h]][:, 0:Dh] - ws[h][0:C]).astype(BF16) for h in range(H)]
        o2 = [jnp.dot(intra[idx[h]], vnb[h], preferred_element_type=F32) for h in range(H)]
        ds = []
        for h in range(H):
            i = idx[h]
            kd = (kf[i] * jnp.exp(gc[i][C - 1:C, :] - gc[i])).astype(BF16)
            ds.append(lax.dot_general(kd, vnb[h], TN, preferred_element_type=F32))
        for h in range(H):
            i = idx[h]
            o_ref[r0:r0 + C, h * Dh:(h + 1) * Dh] = (ws[h][C:2 * C] + o2[h]).astype(BF16)
            state[h] = state[h] * jnp.exp(gc[i][C - 1:C, :]) + ds[h]
    for h in range(H):
        s_ref[h] = state[h]


def _dn_chunk(q, k, v, gcol, grow):
    B, T, W = q.shape
    tc = DNC_CPS * CHUNK
    tok = pl.BlockSpec((None, tc, W), lambda b, t: (b, t, 0))
    return pl.pallas_call(
        _dnchunk_kernel,
        grid=(B, T // tc),
        in_specs=[tok, tok, tok,
                  pl.BlockSpec((None, tc, LANES), lambda b, t: (b, t, 0)),
                  pl.BlockSpec((None, DNC_CPS, 2 * DN_HEADS, CHUNK), lambda b, t: (b, t, 0, 0))],
        out_specs=tok,
        out_shape=jax.ShapeDtypeStruct((B, T, W), BF16),
        scratch_shapes=[pltpu.VMEM((DN_HEADS, DN_HEAD_DIM, DN_HEAD_DIM), F32)],
        compiler_params=_params("arbitrary", "arbitrary"),
        name="dn_chunk",
    )(q, k, v, gcol, grow)


def _dnout_kernel(x_ref, mod_ref, o_ref_in, z_ref, og_ref, wout_ref, out_ref, y_ref):
    Dh = DN_HEAD_DIM
    for h in range(DN_HEADS):
        hs = slice(h * Dh, (h + 1) * Dh)
        o = o_ref_in[:, hs].astype(F32)
        ms = jnp.mean(o * o, axis=-1, keepdims=True)
        y = o * lax.rsqrt(ms + EPS) * og_ref[...] * _silu(z_ref[:, hs].astype(F32))
        y_ref[:, hs] = y.astype(BF16)
    out = jnp.dot(y_ref[...], wout_ref[...], preferred_element_type=F32)
    out_ref[...] = x_ref[...] + (1.0 + mod_ref[2:3, :]) * out


def _dn_out(x, mod, o, z, o_g, w_out):
    B, T, D = x.shape
    W = DN_WIDTH
    return pl.pallas_call(
        _dnout_kernel,
        grid=(B, T // DNO_TM),
        in_specs=[
            pl.BlockSpec((None, DNO_TM, D), lambda b, t: (b, t, 0)),
            pl.BlockSpec((None, None, 3, D), lambda b, t: (b, 1, 0, 0)),
            pl.BlockSpec((None, DNO_TM, W), lambda b, t: (b, t, 0)),
            pl.BlockSpec((None, DNO_TM, W), lambda b, t: (b, t, 0)),
            _const_spec((1, DN_HEAD_DIM)),
            _const_spec((W, D)),
        ],
        out_specs=pl.BlockSpec((None, DNO_TM, D), lambda b, t: (b, t, 0)),
        out_shape=jax.ShapeDtypeStruct((B, T, D), F32),
        scratch_shapes=[pltpu.VMEM((DNO_TM, W), BF16)],
        compiler_params=_params("parallel", "parallel"),
        name="dn_out",
    )(x, mod, o, z, o_g.reshape(1, DN_HEAD_DIM), w_out)


def _lane_pad_row(v):
    return jnp.pad(v.astype(F32), (0, LANES - v.shape[0])).reshape(1, LANES)


def kernel(x, c, norm_g, w_ada, b_ada, w_ffn_in, w_ffn_out, cm_w_glu, cm_b_glu, cm_w_dw, cm_b_dw, cm_ln_g, cm_ln_b, cm_w_pw, cm_b_pw, dn_w_in, dn_w_sconv, dn_a_log, dn_dt_bias, dn_o_g, dn_w_out, final_g):
    B = x.shape[0]
    W = DN_WIDTH
    mod_all = _ada_mod(c, w_ada, b_ada).reshape(DEPTH, B, N_SUB, 3, D_MODEL)
    for i in range(DEPTH):
        mod = mod_all[i]
        x = _ffn(x, mod, 0, norm_g[i, 0], w_ffn_in[i, 0].astype(BF16), w_ffn_out[i, 0].astype(BF16),
                 final_g, False)
        if i % 2 == 0:
            a = i // 2
            x = _conv_module(x, mod, norm_g[i, 1], cm_w_glu[a].astype(BF16), cm_b_glu[a], cm_w_dw[a],
                             cm_b_dw[a], cm_ln_g[a], cm_ln_b[a], cm_w_pw[a].astype(BF16), cm_b_pw[a])
        else:
            m = i // 2
            w_in = dn_w_in[m]
            w_ab = jnp.pad(w_in[:, 4 * W:], ((0, 0), (0, LANES - 2 * DN_HEADS))).astype(BF16)
            q, k, v, z, gcol, grow = _dn_proj(
                x, mod, norm_g[i, 1], w_in[:, :4 * W].astype(BF16), w_ab, dn_w_sconv[m],
                _lane_pad_row(dn_a_log[m]), _lane_pad_row(dn_dt_bias[m]))
            o = _dn_chunk(q, k, v, gcol, grow)
            x = _dn_out(x, mod, o, z, dn_o_g[m], dn_w_out[m].astype(BF16))
        x = _ffn(x, mod, 2, norm_g[i, 2], w_ffn_in[i, 1].astype(BF16), w_ffn_out[i, 1].astype(BF16),
                 final_g, i == DEPTH - 1)
    return x
```

```python
import functools

import jax
import jax.numpy as jnp
from jax import lax
from jax.experimental import pallas as pl
from jax.experimental.pallas import tpu as pltpu

D_MODEL = 1024
DEPTH = 4
N_SUB = 3
D_FF = 2816
CONV_WIDTH = 31
DN_HEADS = 8
DN_HEAD_DIM = 128
DN_WIDTH = DN_HEADS * DN_HEAD_DIM
SHORT_CONV = 4
CHUNK = 64
EPS = 1e-6
FFN_RES_WEIGHT = 0.5

LANES = 128
SUBLANES = 8
VMEM_LIMIT_BYTES = 56 * 1024 * 1024

FFN_TM = 1024
FFN_TF = 256
CONV_TM = 256
CONV_HALO = 32
CONV_RB = 64
CONV_CB = 128
DNP_TM = 256
DNP_HALO = 8
DNC_CPS = 8
MOD_TN = 1024

F32 = jnp.float32
BF16 = jnp.bfloat16


def _params(*sem):
    return pltpu.CompilerParams(dimension_semantics=sem, vmem_limit_bytes=VMEM_LIMIT_BYTES)


def _const_spec(shape):
    nd = len(shape)
    return pl.BlockSpec(shape, lambda *_: (0,) * nd, pipeline_mode=pl.Buffered(1))


def _bdot(a, b):
    return jnp.dot(a.astype(BF16), b.astype(BF16), preferred_element_type=F32)


def _silu(x):
    return x * jax.nn.sigmoid(x)


def _modulated(x, g_ref, mod_ref):
    ms = jnp.mean(x * x, axis=-1, keepdims=True)
    y = x * lax.rsqrt(ms + EPS) * g_ref[...]
    return y * (1.0 + mod_ref[1:2, :]) + mod_ref[0:1, :]


def _mod_kernel(c_ref, w_ref, b_ref, o_ref):
    cs = _silu(c_ref[...])
    o_ref[...] = _bdot(cs, w_ref[...]) + b_ref[...]


def _ada_mod(c, w_ada, b_ada):
    B = c.shape[0]
    n = w_ada.shape[-1]
    return pl.pallas_call(
        _mod_kernel,
        grid=(DEPTH, n // MOD_TN),
        in_specs=[
            pl.BlockSpec((B, D_MODEL), lambda i, j: (0, 0)),
            pl.BlockSpec((None, D_MODEL, MOD_TN), lambda i, j: (i, 0, j)),
            pl.BlockSpec((None, 1, MOD_TN), lambda i, j: (i, 0, j)),
        ],
        out_specs=pl.BlockSpec((None, B, MOD_TN), lambda i, j: (i, 0, j)),
        out_shape=jax.ShapeDtypeStruct((DEPTH, B, n), F32),
        compiler_params=_params("parallel", "parallel"),
        name="ada_mod",
    )(c, w_ada, b_ada.reshape(DEPTH, 1, n))


def _ffn_body(x, mod_ref, g_ref, win_ref, wout_ref, fg_ref, a_ref, final_norm):
    hb = _modulated(x, g_ref, mod_ref).astype(BF16)
    for c in range(D_FF // FFN_TF):
        lo = c * FFN_TF
        gate = jnp.dot(hb, win_ref[:, lo:lo + FFN_TF], preferred_element_type=F32)
        up = jnp.dot(hb, win_ref[:, D_FF + lo:D_FF + lo + FFN_TF], preferred_element_type=F32)
        a_ref[:, lo:lo + FFN_TF] = (_silu(gate) * up).astype(BF16)
    y = jnp.dot(a_ref[...], wout_ref[...], preferred_element_type=F32)
    out = x + (FFN_RES_WEIGHT * (1.0 + mod_ref[2:3, :])) * y
    if final_norm:
        ms = jnp.mean(out * out, axis=-1, keepdims=True)
        out = out * lax.rsqrt(ms + EPS) * fg_ref[...]
    return out


def _ffn_kernel(x_ref, mod_ref, g_ref, win_ref, wout_ref, fg_ref, o_ref, a_ref, *, final_norm):
    o_ref[...] = _ffn_body(x_ref[...], mod_ref, g_ref, win_ref, wout_ref, fg_ref, a_ref, final_norm)


def _ffn(x, mod, sub, norm_g, w_in, w_out, final_g, final_norm):
    B, T, D = x.shape
    return pl.pallas_call(
        functools.partial(_ffn_kernel, final_norm=final_norm),
        grid=(B, T // FFN_TM),
        in_specs=[
            pl.BlockSpec((None, FFN_TM, D), lambda b, t: (b, t, 0)),
            pl.BlockSpec((None, None, 3, D), lambda b, t: (b, sub, 0, 0)),
            _const_spec((1, D)),
            _const_spec((D, 2 * D_FF)),
            _const_spec((D_FF, D)),
            _const_spec((1, D)),
        ],
        out_specs=pl.BlockSpec((None, FFN_TM, D), lambda b, t: (b, t, 0)),
        out_shape=jax.ShapeDtypeStruct((B, T, D), F32),
        scratch_shapes=[pltpu.VMEM((FFN_TM, D_FF), BF16)],
        compiler_params=_params("parallel", "parallel"),
        name="ffn",
    )(x, mod, norm_g.reshape(1, D), w_in, w_out, final_g.reshape(1, D))


def _conv_kernel(x_ref, mod_ref, g_ref, wglu_ref, bglu_ref, wdw_ref, bdw_ref, lng_ref, lnb_ref,
                 wpw_ref, bpw_ref, o_ref, u_ref, c_ref):
    D = D_MODEL
    tm = CONV_TM

    @pl.when(pl.program_id(1) == 0)
    def _():
        u_ref[0:CONV_HALO, :] = jnp.zeros((CONV_HALO, D), F32)

    x = x_ref[...]
    hb = _modulated(x, g_ref, mod_ref).astype(BF16)
    a = jnp.dot(hb, wglu_ref[:, 0:D], preferred_element_type=F32) + bglu_ref[:, 0:D]
    b = jnp.dot(hb, wglu_ref[:, D:2 * D], preferred_element_type=F32) + bglu_ref[:, D:2 * D]
    u_ref[CONV_HALO:CONV_HALO + tm, :] = a * jax.nn.sigmoid(b)

    base = CONV_HALO - (CONV_WIDTH - 1)
    for rb in range(tm // CONV_RB):
        r0 = rb * CONV_RB
        for cb in range(D // CONV_CB):
            cs = slice(cb * CONV_CB, (cb + 1) * CONV_CB)
            acc = None
            for s in range(SUBLANES):
                taps = [k for k in range(CONV_WIDTH) if (base + k) % SUBLANES == s]
                rows = (base + taps[-1]) // SUBLANES * SUBLANES + CONV_RB
                slab = u_ref[r0 + s:r0 + s + rows, cs]
                part = None
                for k in taps:
                    j0 = (base + k) // SUBLANES * SUBLANES
                    term = wdw_ref[k:k + 1, cs] * slab[j0:j0 + CONV_RB]
                    part = term if part is None else part + term
                acc = part if acc is None else acc + part
            c_ref[r0:r0 + CONV_RB, cs] = acc
    u_ref[0:CONV_HALO, :] = u_ref[tm:tm + CONV_HALO, :]

    cv = c_ref[...] + bdw_ref[...]
    mu = jnp.mean(cv, axis=-1, keepdims=True)
    xc = cv - mu
    var = jnp.mean(xc * xc, axis=-1, keepdims=True)
    y = xc * lax.rsqrt(var + EPS) * lng_ref[...] + lnb_ref[...]
    out = _bdot(_silu(y), wpw_ref[...]) + bpw_ref[...]
    o_ref[...] = x + (1.0 + mod_ref[2:3, :]) * out


def _conv_module(x, mod, norm_g, w_glu, b_glu, w_dw, b_dw, ln_g, ln_b, w_pw, b_pw):
    B, T, D = x.shape
    return pl.pallas_call(
        _conv_kernel,
        grid=(B, T // CONV_TM),
        in_specs=[
            pl.BlockSpec((None, CONV_TM, D), lambda b, t: (b, t, 0)),
            pl.BlockSpec((None, None, 3, D), lambda b, t: (b, 1, 0, 0)),
            _const_spec((1, D)),
            _const_spec((D, 2 * D)),
            _const_spec((1, 2 * D)),
            _const_spec((CONV_WIDTH, D)),
            _const_spec((1, D)),
            _const_spec((1, D)),
            _const_spec((1, D)),
            _const_spec((D, D)),
            _const_spec((1, D)),
        ],
        out_specs=pl.BlockSpec((None, CONV_TM, D), lambda b, t: (b, t, 0)),
        out_shape=jax.ShapeDtypeStruct((B, T, D), F32),
        scratch_shapes=[pltpu.VMEM((CONV_HALO + CONV_TM, D), F32), pltpu.VMEM((CONV_TM, D), F32)],
        compiler_params=_params("arbitrary", "arbitrary"),
        name="conv_module",
    )(x, mod, norm_g.reshape(1, D), w_glu, b_glu.reshape(1, 2 * D), w_dw, b_dw.reshape(1, D),
      ln_g.reshape(1, D), ln_b.reshape(1, D), w_pw, b_pw.reshape(1, D))


def _dnproj_kernel(x_ref, mod_ref, g_ref, wqkvz_ref, wab_ref, wsc_ref, alog_ref, dtb_ref,
                   q_ref, k_ref, v_ref, z_ref, gcol_ref, grow_ref, pre_ref):
    W = DN_WIDTH
    tm = DNP_TM

    @pl.when(pl.program_id(1) == 0)
    def _():
        pre_ref[0:DNP_HALO, :] = jnp.zeros((DNP_HALO, 3 * W), F32)

    hb = _modulated(x_ref[...], g_ref, mod_ref).astype(BF16)
    z_ref[...] = jnp.dot(hb, wqkvz_ref[:, 3 * W:4 * W], preferred_element_type=F32).astype(BF16)

    ab = jnp.dot(hb, wab_ref[...], preferred_element_type=F32)
    lane = lax.broadcasted_iota(jnp.int32, (tm, LANES), 1)
    g = -jnp.exp(alog_ref[...]) * jax.nn.softplus(ab + dtb_ref[...])
    g = jnp.where(lane < DN_HEADS, g, 0.0)
    ri = lax.broadcasted_iota(jnp.int32, (tm, tm), 0)
    ci = lax.broadcasted_iota(jnp.int32, (tm, tm), 1)
    tri = jnp.where((ri // CHUNK == ci // CHUNK) & (ci <= ri), 1.0, 0.0).astype(F32)
    gc = jnp.dot(tri, g, preferred_element_type=F32, precision=lax.Precision.HIGHEST)
    col = jnp.where(lane < DN_HEADS, gc, jax.nn.sigmoid(ab))
    gcol_ref[...] = col
    colt = col.T
    for c in range(tm // CHUNK):
        grow_ref[c] = colt[0:2 * DN_HEADS, c * CHUNK:(c + 1) * CHUNK]

    for j in range(3):
        pre_ref[DNP_HALO:DNP_HALO + tm, j * W:(j + 1) * W] = jnp.dot(
            hb, wqkvz_ref[:, j * W:(j + 1) * W], preferred_element_type=F32)
    base = DNP_HALO - (SHORT_CONV - 1)
    outs = (q_ref, k_ref, v_ref)
    for j in range(3):
        for h in range(DN_HEADS):
            c0 = j * W + h * DN_HEAD_DIM
            acc = jnp.zeros((tm, DN_HEAD_DIM), F32)
            for k in range(SHORT_CONV):
                acc = acc + wsc_ref[k:k + 1, c0:c0 + DN_HEAD_DIM] * pre_ref[base + k:base + k + tm, c0:c0 + DN_HEAD_DIM]
            y = _silu(acc)
            if j < 2:
                y = y * lax.rsqrt(jnp.sum(y * y, axis=-1, keepdims=True) + EPS)
                if j == 0:
                    y = y * (DN_HEAD_DIM ** -0.5)
            outs[j][:, h * DN_HEAD_DIM:(h + 1) * DN_HEAD_DIM] = y.astype(BF16)
    pre_ref[0:DNP_HALO, :] = pre_ref[tm:tm + DNP_HALO, :]


def _dn_proj(x, mod, norm_g, w_qkvz, w_ab, w_sconv, a_log, dt_bias):
    B, T, D = x.shape
    W = DN_WIDTH
    tok = pl.BlockSpec((None, DNP_TM, W), lambda b, t: (b, t, 0))
    cps = DNP_TM // CHUNK
    return pl.pallas_call(
        _dnproj_kernel,
        grid=(B, T // DNP_TM),
        in_specs=[
            pl.BlockSpec((None, DNP_TM, D), lambda b, t: (b, t, 0)),
            pl.BlockSpec((None, None, 3, D), lambda b, t: (b, 1, 0, 0)),
            _const_spec((1, D)),
            _const_spec((D, 4 * W)),
            _const_spec((D, LANES)),
            _const_spec((SHORT_CONV, 3 * W)),
            _const_spec((1, LANES)),
            _const_spec((1, LANES)),
        ],
        out_specs=[tok, tok, tok, tok,
                   pl.BlockSpec((None, DNP_TM, LANES), lambda b, t: (b, t, 0)),
                   pl.BlockSpec((None, cps, 2 * DN_HEADS, CHUNK), lambda b, t: (b, t, 0, 0))],
        out_shape=[jax.ShapeDtypeStruct((B, T, W), BF16)] * 4 + [
            jax.ShapeDtypeStruct((B, T, LANES), F32),
            jax.ShapeDtypeStruct((B, T // CHUNK, 2 * DN_HEADS, CHUNK), F32)],
        scratch_shapes=[pltpu.VMEM((DNP_HALO + DNP_TM, 3 * W), F32)],
        compiler_params=_params("arbitrary", "arbitrary"),
        name="dn_proj",
    )(x, mod, norm_g.reshape(1, D), w_qkvz, w_ab, w_sconv, a_log, dt_bias)


def _dnchunk_kernel(q_ref, k_ref, v_ref, gcol_ref, grow_ref, o_ref, s_ref):
    C = CHUNK
    Dh = DN_HEAD_DIM
    H = DN_HEADS
    NT = (((1,), (1,)), ((), ()))
    TN = (((0,), (0,)), ((), ()))

    @pl.when(pl.program_id(1) == 0)
    def _():
        s_ref[...] = jnp.zeros(s_ref.shape, F32)

    ri = lax.broadcasted_iota(jnp.int32, (C, C), 0)
    ci = lax.broadcasted_iota(jnp.int32, (C, C), 1)
    causal = ci <= ri
    strict = ci < ri
    eye = jnp.where(ri == ci, 1.0, 0.0).astype(F32)
    same = {m: lax.shift_right_logical(ri, s) == lax.shift_right_logical(ci, s)
            for m, s in ((8, 3), (16, 4), (32, 5))}
    off = {8: same[16] & jnp.logical_not(same[8]),
           16: same[32] & jnp.logical_not(same[16]),
           32: jnp.logical_not(same[32])}

    items = [(c, h) for c in range(DNC_CPS) for h in range(H)]
    n = len(items)

    k16, q16, kf, beta, gc, eg, decay, kbeta = ([None] * n for _ in range(8))
    for i, (c, h) in enumerate(items):
        r0 = c * C
        hs = slice(h * Dh, (h + 1) * Dh)
        k16[i] = k_ref[r0:r0 + C, hs]
        q16[i] = q_ref[r0:r0 + C, hs]
        kf[i] = k16[i].astype(F32)
        gc[i] = gcol_ref[r0:r0 + C, h:h + 1]
        beta[i] = gcol_ref[r0:r0 + C, H + h:H + h + 1]
        gr = grow_ref[c, h:h + 1, :]
        decay[i] = jnp.where(causal, jnp.exp(jnp.where(causal, gc[i] - gr, 0.0)), 0.0)
        kbeta[i] = kf[i] * beta[i]
        eg[i] = jnp.exp(gc[i])

    kq = [lax.dot_general(jnp.concatenate([kbeta[i].astype(BF16), q16[i]], axis=0), k16[i], NT,
                          preferred_element_type=F32) for i in range(n)]
    a_mat = [jnp.where(strict, kq[i][0:C] * decay[i], 0.0) for i in range(n)]
    intra = [(kq[i][C:2 * C] * decay[i]).astype(BF16) for i in range(n)]

    d8 = [jnp.where(same[8], a_mat[i], 0.0) for i in range(n)]
    d8_2 = [_bdot(d8[i], d8[i]) for i in range(n)]
    d8_4 = [_bdot(d8_2[i], d8_2[i]) for i in range(n)]
    p8 = [_bdot(eye - d8[i], eye + d8_2[i]) for i in range(n)]
    tinv = [_bdot(p8[i], eye + d8_4[i]) for i in range(n)]
    for m in (8, 16, 32):
        et = [_bdot(jnp.where(off[m], a_mat[i], 0.0), tinv[i]) for i in range(n)]
        tinv = [tinv[i] - _bdot(tinv[i], et[i]) for i in range(n)]

    uw = []
    for i, (c, h) in enumerate(items):
        r0 = c * C
        vf = v_ref[r0:r0 + C, h * Dh:(h + 1) * Dh].astype(F32)
        rhs = jnp.concatenate([(vf * beta[i]).astype(BF16), (kbeta[i] * eg[i]).astype(BF16)], axis=1)
        uw.append(jnp.dot(tinv[i].astype(BF16), rhs, preferred_element_type=F32))

    state = [s_ref[h] for h in range(H)]
    for c in range(DNC_CPS):
        r0 = c * C
        idx = [c * H + h for h in range(H)]
        sb = [state[h].astype(BF16) for h in range(H)]
        wq = [jnp.concatenate([uw[i][:, Dh:2 * Dh].astype(BF16),
                               (q16[i].astype(F32) * eg[i]).astype(BF16)], axis=0) for i in idx]
        ws = [jnp.dot(wq[h], sb[h], preferred_element_type=F32) for h in range(H)]
        vnb = [(uw[idx[h]][:, 0:Dh] - ws[h][0:C]).astype(BF16) for h in range(H)]
        o2 = [jnp.dot(intra[idx[h]], vnb[h], preferred_element_type=F32) for h in range(H)]
        ds = []
        for h in range(H):
            i = idx[h]
            kd = (kf[i] * jnp.exp(gc[i][C - 1:C, :] - gc[i])).astype(BF16)
            ds.append(lax.dot_general(kd, vnb[h], TN, preferred_element_type=F32))
        for h in range(H):
            i = idx[h]
            o_ref[r0:r0 + C, h * Dh:(h + 1) * Dh] = (ws[h][C:2 * C] + o2[h]).astype(BF16)
            state[h] = state[h] * jnp.exp(gc[i][C - 1:C, :]) + ds[h]
    for h in range(H):
        s_ref[h] = state[h]


def _dn_chunk(q, k, v, gcol, grow):
    B, T, W = q.shape
    tc = DNC_CPS * CHUNK
    tok = pl.BlockSpec((None, tc, W), lambda b, t: (b, t, 0))
    return pl.pallas_call(
        _dnchunk_kernel,
        grid=(B, T // tc),
        in_specs=[tok, tok, tok,
                  pl.BlockSpec((None, tc, LANES), lambda b, t: (b, t, 0)),
                  pl.BlockSpec((None, DNC_CPS, 2 * DN_HEADS, CHUNK), lambda b, t: (b, t, 0, 0))],
        out_specs=tok,
        out_shape=jax.ShapeDtypeStruct((B, T, W), BF16),
        scratch_shapes=[pltpu.VMEM((DN_HEADS, DN_HEAD_DIM, DN_HEAD_DIM), F32)],
        compiler_params=_params("arbitrary", "arbitrary"),
        name="dn_chunk",
    )(q, k, v, gcol, grow)


def _dnout_ffn_kernel(x_ref, modm_ref, modf_ref, o_ref_in, z_ref, og_ref, wdn_ref, g_ref, win_ref, wout_ref,
                      fg_ref, out_ref, y_ref, a_ref, *, final_norm):
    Dh = DN_HEAD_DIM
    for h in range(DN_HEADS):
        hs = slice(h * Dh, (h + 1) * Dh)
        o = o_ref_in[:, hs].astype(F32)
        ms = jnp.mean(o * o, axis=-1, keepdims=True)
        y = o * lax.rsqrt(ms + EPS) * og_ref[...] * _silu(z_ref[:, hs].astype(F32))
        y_ref[:, hs] = y.astype(BF16)
    mixed = jnp.dot(y_ref[...], wdn_ref[...], preferred_element_type=F32)
    x = x_ref[...] + (1.0 + modm_ref[2:3, :]) * mixed
    out_ref[...] = _ffn_body(x, modf_ref, g_ref, win_ref, wout_ref, fg_ref, a_ref, final_norm)


def _dnout_ffn(x, mod, o, z, o_g, w_dn, norm_g, w_in, w_out, final_g, final_norm):
    B, T, D = x.shape
    W = DN_WIDTH
    tok = pl.BlockSpec((None, FFN_TM, W), lambda b, t: (b, t, 0))
    return pl.pallas_call(
        functools.partial(_dnout_ffn_kernel, final_norm=final_norm),
        grid=(B, T // FFN_TM),
        in_specs=[
            pl.BlockSpec((None, FFN_TM, D), lambda b, t: (b, t, 0)),
            pl.BlockSpec((None, None, 3, D), lambda b, t: (b, 1, 0, 0)),
            pl.BlockSpec((None, None, 3, D), lambda b, t: (b, 2, 0, 0)),
            tok, tok,
            _const_spec((1, DN_HEAD_DIM)),
            _const_spec((W, D)),
            _const_spec((1, D)),
            _const_spec((D, 2 * D_FF)),
            _const_spec((D_FF, D)),
            _const_spec((1, D)),
        ],
        out_specs=pl.BlockSpec((None, FFN_TM, D), lambda b, t: (b, t, 0)),
        out_shape=jax.ShapeDtypeStruct((B, T, D), F32),
        scratch_shapes=[pltpu.VMEM((FFN_TM, W), BF16), pltpu.VMEM((FFN_TM, D_FF), BF16)],
        compiler_params=_params("parallel", "parallel"),
        name="dnout_ffn",
    )(x, mod, mod, o, z, o_g.reshape(1, DN_HEAD_DIM), w_dn, norm_g.reshape(1, D), w_in, w_out,
      final_g.reshape(1, D))


def _lane_pad_row(v):
    return jnp.pad(v.astype(F32), (0, LANES - v.shape[0])).reshape(1, LANES)


def kernel(x, c, norm_g, w_ada, b_ada, w_ffn_in, w_ffn_out, cm_w_glu, cm_b_glu, cm_w_dw, cm_b_dw, cm_ln_g, cm_ln_b, cm_w_pw, cm_b_pw, dn_w_in, dn_w_sconv, dn_a_log, dn_dt_bias, dn_o_g, dn_w_out, final_g):
    B = x.shape[0]
    W = DN_WIDTH
    mod_all = _ada_mod(c, w_ada, b_ada).reshape(DEPTH, B, N_SUB, 3, D_MODEL)
    for i in range(DEPTH):
        mod = mod_all[i]
        x = _ffn(x, mod, 0, norm_g[i, 0], w_ffn_in[i, 0].astype(BF16), w_ffn_out[i, 0].astype(BF16),
                 final_g, False)
        w_in2, w_out2 = w_ffn_in[i, 1].astype(BF16), w_ffn_out[i, 1].astype(BF16)
        if i % 2 == 0:
            a = i // 2
            x = _conv_module(x, mod, norm_g[i, 1], cm_w_glu[a].astype(BF16), cm_b_glu[a], cm_w_dw[a],
                             cm_b_dw[a], cm_ln_g[a], cm_ln_b[a], cm_w_pw[a].astype(BF16), cm_b_pw[a])
            x = _ffn(x, mod, 2, norm_g[i, 2], w_in2, w_out2, final_g, i == DEPTH - 1)
        else:
            m = i // 2
            w_in = dn_w_in[m]
            w_ab = jnp.pad(w_in[:, 4 * W:], ((0, 0), (0, LANES - 2 * DN_HEADS))).astype(BF16)
            q, k, v, z, gcol, grow = _dn_proj(
                x, mod, norm_g[i, 1], w_in[:, :4 * W].astype(BF16), w_ab, dn_w_sconv[m],
                _lane_pad_row(dn_a_log[m]), _lane_pad_row(dn_dt_bias[m]))
            o = _dn_chunk(q, k, v, gcol, grow)
            x = _dnout_ffn(x, mod, o, z, dn_o_g[m], dn_w_out[m].astype(BF16), norm_g[i, 2], w_in2, w_out2,
                           final_g, i == DEPTH - 1)
    return x
```

```python
import functools

import jax
import jax.numpy as jnp
from jax import lax
from jax.experimental import pallas as pl
from jax.experimental.pallas import tpu as pltpu

D_MODEL = 1024
DEPTH = 4
N_SUB = 3
D_FF = 2816
CONV_WIDTH = 31
DN_HEADS = 8
DN_HEAD_DIM = 128
DN_WIDTH = DN_HEADS * DN_HEAD_DIM
SHORT_CONV = 4
CHUNK = 64
EPS = 1e-6
FFN_RES_WEIGHT = 0.5

LANES = 128
SUBLANES = 8
VMEM_LIMIT_BYTES = 56 * 1024 * 1024

FFN_TM = 1024
FFN_TF = 256
FFN_ROW_GROUPS = 4
CONV_TM = 256
CONV_HALO = 32
CONV_RB = 64
CONV_CB = 128
CF_TM = 512
DNP_TM = 256
DNP_HALO = 8
DNC_CPS = 8
MOD_TN = 1024

F32 = jnp.float32
BF16 = jnp.bfloat16


def _params(*sem):
    return pltpu.CompilerParams(dimension_semantics=sem, vmem_limit_bytes=VMEM_LIMIT_BYTES)


def _const_spec(shape):
    nd = len(shape)
    return pl.BlockSpec(shape, lambda *_: (0,) * nd, pipeline_mode=pl.Buffered(1))


def _bdot(a, b):
    return jnp.dot(a.astype(BF16), b.astype(BF16), preferred_element_type=F32)


def _silu(x):
    return x * jax.nn.sigmoid(x)


def _modulated(x, g_ref, mod_ref):
    ms = jnp.mean(x * x, axis=-1, keepdims=True)
    y = x * lax.rsqrt(ms + EPS) * g_ref[...]
    return y * (1.0 + mod_ref[1:2, :]) + mod_ref[0:1, :]


def _mod_kernel(c_ref, w_ref, b_ref, o_ref):
    cs = _silu(c_ref[...])
    o_ref[...] = _bdot(cs, w_ref[...]) + b_ref[...]


def _ada_mod(c, w_ada, b_ada):
    B = c.shape[0]
    n = w_ada.shape[-1]
    return pl.pallas_call(
        _mod_kernel,
        grid=(DEPTH, n // MOD_TN),
        in_specs=[
            pl.BlockSpec((B, D_MODEL), lambda i, j: (0, 0)),
            pl.BlockSpec((None, D_MODEL, MOD_TN), lambda i, j: (i, 0, j)),
            pl.BlockSpec((None, 1, MOD_TN), lambda i, j: (i, 0, j)),
        ],
        out_specs=pl.BlockSpec((None, B, MOD_TN), lambda i, j: (i, 0, j)),
        out_shape=jax.ShapeDtypeStruct((DEPTH, B, n), F32),
        compiler_params=_params("parallel", "parallel"),
        name="ada_mod",
    )(c, w_ada, b_ada.reshape(DEPTH, 1, n))


def _ffn_body(x_rows, mod_ref, g_ref, win_ref, wout_ref, fg_ref, a_ref, final_norm):
    def hidden(h, lo):
        gate = jnp.dot(h, win_ref[:, lo:lo + FFN_TF], preferred_element_type=F32)
        up = jnp.dot(h, win_ref[:, D_FF + lo:D_FF + lo + FFN_TF], preferred_element_type=F32)
        return (_silu(gate) * up).astype(BF16)

    rows = FFN_TM // FFN_ROW_GROUPS
    xs, hbs = [], []
    for q in range(FFN_ROW_GROUPS):
        rs = slice(q * rows, (q + 1) * rows)
        xs.append(x_rows(rs))
        hbs.append(_modulated(xs[q], g_ref, mod_ref).astype(BF16))
        a_ref[rs, 0:FFN_TF] = hidden(hbs[q], 0)
    x = jnp.concatenate(xs, axis=0)
    hb = jnp.concatenate(hbs, axis=0)
    for c in range(1, D_FF // FFN_TF):
        a_ref[:, c * FFN_TF:(c + 1) * FFN_TF] = hidden(hb, c * FFN_TF)
    y = jnp.dot(a_ref[...], wout_ref[...], preferred_element_type=F32)
    out = x + (FFN_RES_WEIGHT * (1.0 + mod_ref[2:3, :])) * y
    if final_norm:
        ms = jnp.mean(out * out, axis=-1, keepdims=True)
        out = out * lax.rsqrt(ms + EPS) * fg_ref[...]
    return out


def _ffn_kernel(x_ref, mod_ref, g_ref, win_ref, wout_ref, fg_ref, o_ref, a_ref, *, final_norm):
    o_ref[...] = _ffn_body(lambda rs: x_ref[rs, :], mod_ref, g_ref, win_ref, wout_ref, fg_ref, a_ref, final_norm)


def _ffn(x, mod, sub, norm_g, w_in, w_out, final_g, final_norm):
    B, T, D = x.shape
    return pl.pallas_call(
        functools.partial(_ffn_kernel, final_norm=final_norm),
        grid=(B, T // FFN_TM),
        in_specs=[
            pl.BlockSpec((None, FFN_TM, D), lambda b, t: (b, t, 0)),
            pl.BlockSpec((None, None, 3, D), lambda b, t: (b, sub, 0, 0)),
            _const_spec((1, D)),
            _const_spec((D, 2 * D_FF)),
            _const_spec((D_FF, D)),
            _const_spec((1, D)),
        ],
        out_specs=pl.BlockSpec((None, FFN_TM, D), lambda b, t: (b, t, 0)),
        out_shape=jax.ShapeDtypeStruct((B, T, D), F32),
        scratch_shapes=[pltpu.VMEM((FFN_TM, D_FF), BF16)],
        compiler_params=_params("parallel", "parallel"),
        name="ffn",
    )(x, mod, norm_g.reshape(1, D), w_in, w_out, final_g.reshape(1, D))


def _conv_kernel(x_ref, mod_ref, g_ref, wglu_ref, bglu_ref, wdw_ref, bdw_ref, lng_ref, lnb_ref,
                 wpw_ref, bpw_ref, o_ref, u_ref, c_ref):
    D = D_MODEL
    tm = CONV_TM

    @pl.when(pl.program_id(1) == 0)
    def _():
        u_ref[0:CONV_HALO, :] = jnp.zeros((CONV_HALO, D), F32)

    x = x_ref[...]
    hb = _modulated(x, g_ref, mod_ref).astype(BF16)
    a = jnp.dot(hb, wglu_ref[:, 0:D], preferred_element_type=F32) + bglu_ref[:, 0:D]
    b = jnp.dot(hb, wglu_ref[:, D:2 * D], preferred_element_type=F32) + bglu_ref[:, D:2 * D]
    u_ref[CONV_HALO:CONV_HALO + tm, :] = a * jax.nn.sigmoid(b)

    base = CONV_HALO - (CONV_WIDTH - 1)
    for rb in range(tm // CONV_RB):
        r0 = rb * CONV_RB
        for cb in range(D // CONV_CB):
            cs = slice(cb * CONV_CB, (cb + 1) * CONV_CB)
            acc = None
            for s in range(SUBLANES):
                taps = [k for k in range(CONV_WIDTH) if (base + k) % SUBLANES == s]
                rows = (base + taps[-1]) // SUBLANES * SUBLANES + CONV_RB
                slab = u_ref[r0 + s:r0 + s + rows, cs]
                part = None
                for k in taps:
                    j0 = (base + k) // SUBLANES * SUBLANES
                    term = wdw_ref[k:k + 1, cs] * slab[j0:j0 + CONV_RB]
                    part = term if part is None else part + term
                acc = part if acc is None else acc + part
            c_ref[r0:r0 + CONV_RB, cs] = acc
    u_ref[0:CONV_HALO, :] = u_ref[tm:tm + CONV_HALO, :]

    cv = c_ref[...] + bdw_ref[...]
    mu = jnp.mean(cv, axis=-1, keepdims=True)
    xc = cv - mu
    var = jnp.mean(xc * xc, axis=-1, keepdims=True)
    y = xc * lax.rsqrt(var + EPS) * lng_ref[...] + lnb_ref[...]
    out = _bdot(_silu(y), wpw_ref[...]) + bpw_ref[...]
    o_ref[...] = x + (1.0 + mod_ref[2:3, :]) * out


def _conv_module(x, mod, norm_g, w_glu, b_glu, w_dw, b_dw, ln_g, ln_b, w_pw, b_pw):
    B, T, D = x.shape
    return pl.pallas_call(
        _conv_kernel,
        grid=(B, T // CONV_TM),
        in_specs=[
            pl.BlockSpec((None, CONV_TM, D), lambda b, t: (b, t, 0)),
            pl.BlockSpec((None, None, 3, D), lambda b, t: (b, 1, 0, 0)),
            _const_spec((1, D)),
            _const_spec((D, 2 * D)),
            _const_spec((1, 2 * D)),
            _const_spec((CONV_WIDTH, D)),
            _const_spec((1, D)),
            _const_spec((1, D)),
            _const_spec((1, D)),
            _const_spec((D, D)),
            _const_spec((1, D)),
        ],
        out_specs=pl.BlockSpec((None, CONV_TM, D), lambda b, t: (b, t, 0)),
        out_shape=jax.ShapeDtypeStruct((B, T, D), F32),
        scratch_shapes=[pltpu.VMEM((CONV_HALO + CONV_TM, D), F32), pltpu.VMEM((CONV_TM, D), F32)],
        compiler_params=_params("arbitrary", "arbitrary"),
        name="conv_module",
    )(x, mod, norm_g.reshape(1, D), w_glu, b_glu.reshape(1, 2 * D), w_dw, b_dw.reshape(1, D),
      ln_g.reshape(1, D), ln_b.reshape(1, D), w_pw, b_pw.reshape(1, D))


def _fir_lane_block(wdw_ref, u_ref, c_ref, slot, cs):
    base = CONV_HALO - (CONV_WIDTH - 1)
    for rb in range(CF_TM // CONV_RB):
        r0 = rb * CONV_RB
        acc = None
        for s in range(SUBLANES):
            taps = [k for k in range(CONV_WIDTH) if (base + k) % SUBLANES == s]
            rows = (base + taps[-1]) // SUBLANES * SUBLANES + CONV_RB
            slab = u_ref[r0 + s:r0 + s + rows, cs]
            part = None
            for k in taps:
                j0 = (base + k) // SUBLANES * SUBLANES
                term = wdw_ref[k:k + 1, cs] * slab[j0:j0 + CONV_RB]
                part = term if part is None else part + term
            acc = part if acc is None else acc + part
        c_ref[slot, r0:r0 + CONV_RB, cs] = acc


def _convffn_kernel(x_ref, modc_ref, modp_ref, modf_ref, gc_ref, wglu_ref, bglu_ref, wdw_ref, bdw_ref,
                    lng_ref, lnb_ref, wpw_ref, bpw_ref, gf_ref, win_ref, wout_ref, fg_ref,
                    o_ref, u_ref, c_ref, xs_ref, xm_ref, hb_ref, acc_ref, *, tiles_per_seq, final_norm):
    D = D_MODEL
    tm = CF_TM
    i = pl.program_id(0)

    @pl.when(i == 0)
    def _():
        u_ref[...] = jnp.zeros(u_ref.shape, F32)
        c_ref[...] = jnp.zeros(c_ref.shape, F32)
        xs_ref[...] = jnp.zeros(xs_ref.shape, F32)

    slot = lax.rem(i, 2)

    x = x_ref[...]
    xs_ref[slot] = x
    first = lax.rem(i, tiles_per_seq) == 0
    u_ref[0:CONV_HALO, :] = jnp.where(first, 0.0, u_ref[tm:tm + CONV_HALO, :])
    hc = _modulated(x, gc_ref, modc_ref).astype(BF16)
    a = jnp.dot(hc, wglu_ref[:, 0:D], preferred_element_type=F32) + bglu_ref[:, 0:D]
    b = jnp.dot(hc, wglu_ref[:, D:2 * D], preferred_element_type=F32) + bglu_ref[:, D:2 * D]
    u_ref[CONV_HALO:CONV_HALO + tm, :] = a * jax.nn.sigmoid(b)

    cv = c_ref[1 - slot] + bdw_ref[...]
    mu = jnp.mean(cv, axis=-1, keepdims=True)
    xc = cv - mu
    var = jnp.mean(xc * xc, axis=-1, keepdims=True)
    y = xc * lax.rsqrt(var + EPS) * lng_ref[...] + lnb_ref[...]
    mixed = _bdot(_silu(y), wpw_ref[...]) + bpw_ref[...]
    xm = xs_ref[1 - slot] + (1.0 + modp_ref[2:3, :]) * mixed
    xm_ref[...] = xm
    hb_ref[...] = _modulated(xm, gf_ref, modf_ref).astype(BF16)
    acc_ref[...] = jnp.zeros(acc_ref.shape, F32)

    def ffn_chunk(c):
        lo = pl.multiple_of(c * FFN_TF, FFN_TF)
        hb = hb_ref[...]
        gate = jnp.dot(hb, win_ref[:, pl.ds(lo, FFN_TF)], preferred_element_type=F32)
        up = jnp.dot(hb, win_ref[:, pl.ds(D_FF + lo, FFN_TF)], preferred_element_type=F32)
        act = (_silu(gate) * up).astype(BF16)
        acc_ref[...] += jnp.dot(act, wout_ref[pl.ds(lo, FFN_TF), :], preferred_element_type=F32)

    def chunk_and_fir(c, carry):
        ffn_chunk(c)
        _fir_lane_block(wdw_ref, u_ref, c_ref, slot, pl.ds(pl.multiple_of(c * CONV_CB, CONV_CB), CONV_CB))
        return carry

    def chunk_only(c, carry):
        ffn_chunk(c)
        return carry

    n_fir = D // CONV_CB
    lax.fori_loop(0, n_fir, chunk_and_fir, 0)
    lax.fori_loop(n_fir, D_FF // FFN_TF, chunk_only, 0)

    out = xm_ref[...] + (FFN_RES_WEIGHT * (1.0 + modf_ref[2:3, :])) * acc_ref[...]
    if final_norm:
        ms = jnp.mean(out * out, axis=-1, keepdims=True)
        out = out * lax.rsqrt(ms + EPS) * fg_ref[...]
    o_ref[...] = out


def _conv_ffn(x, mod, g_conv, w_glu, b_glu, w_dw, b_dw, ln_g, ln_b, w_pw, b_pw,
              g_ffn, w_in, w_out, final_g, final_norm):
    B, T, D = x.shape
    tm = CF_TM
    tps = T // tm
    n = B * tps

    def cur(i):
        return jnp.minimum(i, n - 1)

    def prev(i):
        return jnp.maximum(i - 1, 0)

    out = pl.pallas_call(
        functools.partial(_convffn_kernel, tiles_per_seq=tps, final_norm=final_norm),
        grid=(n + 1,),
        in_specs=[
            pl.BlockSpec((tm, D), lambda i: (cur(i), 0)),
            pl.BlockSpec((None, None, 3, D), lambda i: (cur(i) // tps, 1, 0, 0)),
            pl.BlockSpec((None, None, 3, D), lambda i: (prev(i) // tps, 1, 0, 0)),
            pl.BlockSpec((None, None, 3, D), lambda i: (prev(i) // tps, 2, 0, 0)),
            _const_spec((1, D)),
            _const_spec((D, 2 * D)),
            _const_spec((1, 2 * D)),
            _const_spec((CONV_WIDTH, D)),
            _const_spec((1, D)),
            _const_spec((1, D)),
            _const_spec((1, D)),
            _const_spec((D, D)),
            _const_spec((1, D)),
            _const_spec((1, D)),
            _const_spec((D, 2 * D_FF)),
            _const_spec((D_FF, D)),
            _const_spec((1, D)),
        ],
        out_specs=pl.BlockSpec((tm, D), lambda i: (prev(i), 0)),
        out_shape=jax.ShapeDtypeStruct((B * T, D), F32),
        scratch_shapes=[pltpu.VMEM((CONV_HALO + tm, D), F32), pltpu.VMEM((2, tm, D), F32),
                        pltpu.VMEM((2, tm, D), F32), pltpu.VMEM((tm, D), F32),
                        pltpu.VMEM((tm, D), BF16), pltpu.VMEM((tm, D), F32)],
        compiler_params=_params("arbitrary"),
        name="conv_ffn",
    )(x.reshape(B * T, D), mod, mod, mod, g_conv.reshape(1, D), w_glu, b_glu.reshape(1, 2 * D), w_dw,
      b_dw.reshape(1, D), ln_g.reshape(1, D), ln_b.reshape(1, D), w_pw, b_pw.reshape(1, D),
      g_ffn.reshape(1, D), w_in, w_out, final_g.reshape(1, D))
    return out.reshape(B, T, D)


def _dnproj_kernel(x_ref, mod_ref, g_ref, wqkvz_ref, wab_ref, wsc_ref, alog_ref, dtb_ref,
                   q_ref, k_ref, v_ref, z_ref, gcol_ref, grow_ref, pre_ref):
    W = DN_WIDTH
    tm = DNP_TM

    @pl.when(pl.program_id(1) == 0)
    def _():
        pre_ref[0:DNP_HALO, :] = jnp.zeros((DNP_HALO, 3 * W), F32)

    hb = _modulated(x_ref[...], g_ref, mod_ref).astype(BF16)
    z_ref[...] = jnp.dot(hb, wqkvz_ref[:, 3 * W:4 * W], preferred_element_type=F32).astype(BF16)

    ab = jnp.dot(hb, wab_ref[...], preferred_element_type=F32)
    lane = lax.broadcasted_iota(jnp.int32, (tm, LANES), 1)
    g = -jnp.exp(alog_ref[...]) * jax.nn.softplus(ab + dtb_ref[...])
    g = jnp.where(lane < DN_HEADS, g, 0.0)
    ri = lax.broadcasted_iota(jnp.int32, (tm, tm), 0)
    ci = lax.broadcasted_iota(jnp.int32, (tm, tm), 1)
    tri = jnp.where((ri // CHUNK == ci // CHUNK) & (ci <= ri), 1.0, 0.0).astype(F32)
    gc = jnp.dot(tri, g, preferred_element_type=F32, precision=lax.Precision.HIGHEST)
    col = jnp.where(lane < DN_HEADS, gc, jax.nn.sigmoid(ab))
    gcol_ref[...] = col
    colt = col.T
    for c in range(tm // CHUNK):
        grow_ref[c] = colt[0:2 * DN_HEADS, c * CHUNK:(c + 1) * CHUNK]

    for j in range(3):
        pre_ref[DNP_HALO:DNP_HALO + tm, j * W:(j + 1) * W] = jnp.dot(
            hb, wqkvz_ref[:, j * W:(j + 1) * W], preferred_element_type=F32)
    base = DNP_HALO - (SHORT_CONV - 1)
    outs = (q_ref, k_ref, v_ref)
    for j in range(3):
        for h in range(DN_HEADS):
            c0 = j * W + h * DN_HEAD_DIM
            acc = jnp.zeros((tm, DN_HEAD_DIM), F32)
            for k in range(SHORT_CONV):
                acc = acc + wsc_ref[k:k + 1, c0:c0 + DN_HEAD_DIM] * pre_ref[base + k:base + k + tm, c0:c0 + DN_HEAD_DIM]
            y = _silu(acc)
            if j < 2:
                y = y * lax.rsqrt(jnp.sum(y * y, axis=-1, keepdims=True) + EPS)
                if j == 0:
                    y = y * (DN_HEAD_DIM ** -0.5)
            outs[j][:, h * DN_HEAD_DIM:(h + 1) * DN_HEAD_DIM] = y.astype(BF16)
    pre_ref[0:DNP_HALO, :] = pre_ref[tm:tm + DNP_HALO, :]


def _dn_proj(x, mod, norm_g, w_qkvz, w_ab, w_sconv, a_log, dt_bias):
    B, T, D = x.shape
    W = DN_WIDTH
    tok = pl.BlockSpec((None, DNP_TM, W), lambda b, t: (b, t, 0))
    cps = DNP_TM // CHUNK
    return pl.pallas_call(
        _dnproj_kernel,
        grid=(B, T // DNP_TM),
        in_specs=[
            pl.BlockSpec((None, DNP_TM, D), lambda b, t: (b, t, 0)),
            pl.BlockSpec((None, None, 3, D), lambda b, t: (b, 1, 0, 0)),
            _const_spec((1, D)),
            _const_spec((D, 4 * W)),
            _const_spec((D, LANES)),
            _const_spec((SHORT_CONV, 3 * W)),
            _const_spec((1, LANES)),
            _const_spec((1, LANES)),
        ],
        out_specs=[tok, tok, tok, tok,
                   pl.BlockSpec((None, DNP_TM, LANES), lambda b, t: (b, t, 0)),
                   pl.BlockSpec((None, cps, 2 * DN_HEADS, CHUNK), lambda b, t: (b, t, 0, 0))],
        out_shape=[jax.ShapeDtypeStruct((B, T, W), BF16)] * 4 + [
            jax.ShapeDtypeStruct((B, T, LANES), F32),
            jax.ShapeDtypeStruct((B, T // CHUNK, 2 * DN_HEADS, CHUNK), F32)],
        scratch_shapes=[pltpu.VMEM((DNP_HALO + DNP_TM, 3 * W), F32)],
        compiler_params=_params("arbitrary", "arbitrary"),
        name="dn_proj",
    )(x, mod, norm_g.reshape(1, D), w_qkvz, w_ab, w_sconv, a_log, dt_bias)


def _dnchunk_kernel(q_ref, k_ref, v_ref, gcol_ref, grow_ref, o_ref, s_ref):
    C = CHUNK
    Dh = DN_HEAD_DIM
    H = DN_HEADS
    NT = (((1,), (1,)), ((), ()))
    TN = (((0,), (0,)), ((), ()))

    @pl.when(pl.program_id(1) == 0)
    def _():
        s_ref[...] = jnp.zeros(s_ref.shape, F32)

    ri = lax.broadcasted_iota(jnp.int32, (C, C), 0)
    ci = lax.broadcasted_iota(jnp.int32, (C, C), 1)
    causal = ci <= ri
    strict = ci < ri
    eye = jnp.where(ri == ci, 1.0, 0.0).astype(F32)
    same = {m: lax.shift_right_logical(ri, s) == lax.shift_right_logical(ci, s)
            for m, s in ((8, 3), (16, 4), (32, 5))}
    off = {8: same[16] & jnp.logical_not(same[8]),
           16: same[32] & jnp.logical_not(same[16]),
           32: jnp.logical_not(same[32])}

    items = [(c, h) for c in range(DNC_CPS) for h in range(H)]
    n = len(items)

    k16, q16, kf, beta, gc, eg, decay, kbeta = ([None] * n for _ in range(8))
    for i, (c, h) in enumerate(items):
        r0 = c * C
        hs = slice(h * Dh, (h + 1) * Dh)
        k16[i] = k_ref[r0:r0 + C, hs]
        q16[i] = q_ref[r0:r0 + C, hs]
        kf[i] = k16[i].astype(F32)
        gc[i] = gcol_ref[r0:r0 + C, h:h + 1]
        beta[i] = gcol_ref[r0:r0 + C, H + h:H + h + 1]
        gr = grow_ref[c, h:h + 1, :]
        decay[i] = jnp.where(causal, jnp.exp(jnp.where(causal, gc[i] - gr, 0.0)), 0.0)
        kbeta[i] = kf[i] * beta[i]
        eg[i] = jnp.exp(gc[i])

    kq = [lax.dot_general(jnp.concatenate([kbeta[i].astype(BF16), q16[i]], axis=0), k16[i], NT,
                          preferred_element_type=F32) for i in range(n)]
    a_mat = [jnp.where(strict, kq[i][0:C] * decay[i], 0.0) for i in range(n)]
    intra = [(kq[i][C:2 * C] * decay[i]).astype(BF16) for i in range(n)]

    d8 = [jnp.where(same[8], a_mat[i], 0.0) for i in range(n)]
    d8_2 = [_bdot(d8[i], d8[i]) for i in range(n)]
    d8_4 = [_bdot(d8_2[i], d8_2[i]) for i in range(n)]
    p8 = [_bdot(eye - d8[i], eye + d8_2[i]) for i in range(n)]
    tinv = [_bdot(p8[i], eye + d8_4[i]) for i in range(n)]
    for m in (8, 16, 32):
        et = [_bdot(jnp.where(off[m], a_mat[i], 0.0), tinv[i]) for i in range(n)]
        tinv = [tinv[i] - _bdot(tinv[i], et[i]) for i in range(n)]

    uw = []
    for i, (c, h) in enumerate(items):
        r0 = c * C
        vf = v_ref[r0:r0 + C, h * Dh:(h + 1) * Dh].astype(F32)
        rhs = jnp.concatenate([(vf * beta[i]).astype(BF16), (kbeta[i] * eg[i]).astype(BF16)], axis=1)
        uw.append(jnp.dot(tinv[i].astype(BF16), rhs, preferred_element_type=F32))

    state = [s_ref[h] for h in range(H)]
    for c in range(DNC_CPS):
        r0 = c * C
        idx = [c * H + h for h in range(H)]
        sb = [state[h].astype(BF16) for h in range(H)]
        wq = [jnp.concatenate([uw[i][:, Dh:2 * Dh].astype(BF16),
                               (q16[i].astype(F32) * eg[i]).astype(BF16)], axis=0) for i in idx]
        ws = [jnp.dot(wq[h], sb[h], preferred_element_type=F32) for h in range(H)]
        vnb = [(uw[idx[h]][:, 0:Dh] - ws[h][0:C]).astype(BF16) for h in range(H)]
        o2 = [jnp.dot(intra[idx[h]], vnb[h], preferred_element_type=F32) for h in range(H)]
        ds = []
        for h in range(H):
            i = idx[h]
            kd = (kf[i] * jnp.exp(gc[i][C - 1:C, :] - gc[i])).astype(BF16)
            ds.append(lax.dot_general(kd, vnb[h], TN, preferred_element_type=F32))
        for h in range(H):
            i = idx[h]
            o_ref[r0:r0 + C, h * Dh:(h + 1) * Dh] = (ws[h][C:2 * C] + o2[h]).astype(BF16)
            state[h] = state[h] * jnp.exp(gc[i][C - 1:C, :]) + ds[h]
    for h in range(H):
        s_ref[h] = state[h]


def _dn_chunk(q, k, v, gcol, grow):
    B, T, W = q.shape
    tc = DNC_CPS * CHUNK
    tok = pl.BlockSpec((None, tc, W), lambda b, t: (b, t, 0))
    return pl.pallas_call(
        _dnchunk_kernel,
        grid=(B, T // tc),
        in_specs=[tok, tok, tok,
                  pl.BlockSpec((None, tc, LANES), lambda b, t: (b, t, 0)),
                  pl.BlockSpec((None, DNC_CPS, 2 * DN_HEADS, CHUNK), lambda b, t: (b, t, 0, 0))],
        out_specs=tok,
        out_shape=jax.ShapeDtypeStruct((B, T, W), BF16),
        scratch_shapes=[pltpu.VMEM((DN_HEADS, DN_HEAD_DIM, DN_HEAD_DIM), F32)],
        compiler_params=_params("arbitrary", "arbitrary"),
        name="dn_chunk",
    )(q, k, v, gcol, grow)


def _dnout_ffn_kernel(x_ref, modm_ref, modf_ref, o_ref_in, z_ref, og_ref, wdn_ref, g_ref, win_ref, wout_ref,
                      fg_ref, out_ref, y_ref, a_ref, *, final_norm):
    Dh = DN_HEAD_DIM
    for h in range(DN_HEADS):
        hs = slice(h * Dh, (h + 1) * Dh)
        o = o_ref_in[:, hs].astype(F32)
        ms = jnp.mean(o * o, axis=-1, keepdims=True)
        y = o * lax.rsqrt(ms + EPS) * og_ref[...] * _silu(z_ref[:, hs].astype(F32))
        y_ref[:, hs] = y.astype(BF16)
    mixed = jnp.dot(y_ref[...], wdn_ref[...], preferred_element_type=F32)
    x = x_ref[...] + (1.0 + modm_ref[2:3, :]) * mixed
    out_ref[...] = _ffn_body(lambda rs: x[rs, :], modf_ref, g_ref, win_ref, wout_ref, fg_ref, a_ref, final_norm)


def _dnout_ffn(x, mod, o, z, o_g, w_dn, norm_g, w_in, w_out, final_g, final_norm):
    B, T, D = x.shape
    W = DN_WIDTH
    tok = pl.BlockSpec((None, FFN_TM, W), lambda b, t: (b, t, 0))
    return pl.pallas_call(
        functools.partial(_dnout_ffn_kernel, final_norm=final_norm),
        grid=(B, T // FFN_TM),
        in_specs=[
            pl.BlockSpec((None, FFN_TM, D), lambda b, t: (b, t, 0)),
            pl.BlockSpec((None, None, 3, D), lambda b, t: (b, 1, 0, 0)),
            pl.BlockSpec((None, None, 3, D), lambda b, t: (b, 2, 0, 0)),
            tok, tok,
            _const_spec((1, DN_HEAD_DIM)),
            _const_spec((W, D)),
            _const_spec((1, D)),
            _const_spec((D, 2 * D_FF)),
            _const_spec((D_FF, D)),
            _const_spec((1, D)),
        ],
        out_specs=pl.BlockSpec((None, FFN_TM, D), lambda b, t: (b, t, 0)),
        out_shape=jax.ShapeDtypeStruct((B, T, D), F32),
        scratch_shapes=[pltpu.VMEM((FFN_TM, W), BF16), pltpu.VMEM((FFN_TM, D_FF), BF16)],
        compiler_params=_params("parallel", "parallel"),
        name="dnout_ffn",
    )(x, mod, mod, o, z, o_g.reshape(1, DN_HEAD_DIM), w_dn, norm_g.reshape(1, D), w_in, w_out,
      final_g.reshape(1, D))


def _lane_pad_row(v):
    return jnp.pad(v.astype(F32), (0, LANES - v.shape[0])).reshape(1, LANES)


def kernel(x, c, norm_g, w_ada, b_ada, w_ffn_in, w_ffn_out, cm_w_glu, cm_b_glu, cm_w_dw, cm_b_dw, cm_ln_g, cm_ln_b, cm_w_pw, cm_b_pw, dn_w_in, dn_w_sconv, dn_a_log, dn_dt_bias, dn_o_g, dn_w_out, final_g):
    B = x.shape[0]
    W = DN_WIDTH
    mod_all = _ada_mod(c, w_ada, b_ada).reshape(DEPTH, B, N_SUB, 3, D_MODEL)
    for i in range(DEPTH):
        mod = mod_all[i]
        x = _ffn(x, mod, 0, norm_g[i, 0], w_ffn_in[i, 0].astype(BF16), w_ffn_out[i, 0].astype(BF16),
                 final_g, False)
        w_in2, w_out2 = w_ffn_in[i, 1].astype(BF16), w_ffn_out[i, 1].astype(BF16)
        if i % 2 == 0:
            a = i // 2
            x = _conv_ffn(x, mod, norm_g[i, 1], cm_w_glu[a].astype(BF16), cm_b_glu[a], cm_w_dw[a],
                          cm_b_dw[a], cm_ln_g[a], cm_ln_b[a], cm_w_pw[a].astype(BF16), cm_b_pw[a],
                          norm_g[i, 2], w_in2, w_out2, final_g, i == DEPTH - 1)
        else:
            m = i // 2
            w_in = dn_w_in[m]
            w_ab = jnp.pad(w_in[:, 4 * W:], ((0, 0), (0, LANES - 2 * DN_HEADS))).astype(BF16)
            q, k, v, z, gcol, grow = _dn_proj(
                x, mod, norm_g[i, 1], w_in[:, :4 * W].astype(BF16), w_ab, dn_w_sconv[m],
                _lane_pad_row(dn_a_log[m]), _lane_pad_row(dn_dt_bias[m]))
            o = _dn_chunk(q, k, v, gcol, grow)
            x = _dnout_ffn(x, mod, o, z, dn_o_g[m], dn_w_out[m].astype(BF16), norm_g[i, 2], w_in2, w_out2,
                           final_g, i == DEPTH - 1)
    return x
```

```python
import functools

import jax
import jax.numpy as jnp
from jax import lax
from jax.experimental import pallas as pl
from jax.experimental.pallas import tpu as pltpu

D_MODEL = 1024
DEPTH = 4
N_SUB = 3
D_FF = 2816
CONV_WIDTH = 31
DN_HEADS = 8
DN_HEAD_DIM = 128
DN_WIDTH = DN_HEADS * DN_HEAD_DIM
SHORT_CONV = 4
CHUNK = 64
EPS = 1e-6
FFN_RES_WEIGHT = 0.5

LANES = 128
SUBLANES = 8
VMEM_LIMIT_BYTES = 56 * 1024 * 1024

FFN_TM = 1024
FFN_TF = 256
FFN_ROW_GROUPS = 4
CONV_TM = 256
CONV_HALO = 32
CONV_RB = 64
CONV_CB = 128
DNP_TM = 256
DNP_HALO = 8
DNC_CPS = 8
MOD_TN = 1024

F32 = jnp.float32
BF16 = jnp.bfloat16


def _params(*sem):
    return pltpu.CompilerParams(dimension_semantics=sem, vmem_limit_bytes=VMEM_LIMIT_BYTES)


def _const_spec(shape, lead=()):
    index = tuple(lead) + (0,) * len(shape)
    return pl.BlockSpec((None,) * len(lead) + tuple(shape), lambda *_: index, pipeline_mode=pl.Buffered(1))


def _mod_spec(layer, sub):
    return pl.BlockSpec((None, None, None, 3, D_MODEL), lambda b, t: (layer, b, sub, 0, 0))


def _bdot(a, b):
    return jnp.dot(a.astype(BF16), b.astype(BF16), preferred_element_type=F32)


def _silu(x):
    return x * jax.nn.sigmoid(x)


def _modulated(x, g_ref, mod_ref):
    ms = jnp.mean(x * x, axis=-1, keepdims=True)
    y = x * lax.rsqrt(ms + EPS) * g_ref[...]
    return y * (1.0 + mod_ref[1:2, :]) + mod_ref[0:1, :]


def _mod_kernel(c_ref, w_ref, b_ref, o_ref):
    cs = _silu(c_ref[...])
    o_ref[...] = _bdot(cs, w_ref[...]) + b_ref[...]


def _ada_mod(c, w_ada, b_ada):
    B = c.shape[0]
    n = w_ada.shape[-1]
    return pl.pallas_call(
        _mod_kernel,
        grid=(DEPTH, n // MOD_TN),
        in_specs=[
            pl.BlockSpec((B, D_MODEL), lambda i, j: (0, 0)),
            pl.BlockSpec((None, D_MODEL, MOD_TN), lambda i, j: (i, 0, j)),
            pl.BlockSpec((None, 1, MOD_TN), lambda i, j: (i, 0, j)),
        ],
        out_specs=pl.BlockSpec((None, B, MOD_TN), lambda i, j: (i, 0, j)),
        out_shape=jax.ShapeDtypeStruct((DEPTH, B, n), F32),
        compiler_params=_params("parallel", "parallel"),
        name="ada_mod",
    )(c, w_ada, b_ada.reshape(DEPTH, 1, n))


def _ffn_body(x_rows, mod_ref, g_ref, win_ref, wout_ref, fg_ref, a_ref, final_norm):
    def hidden(h, lo):
        gate = jnp.dot(h, win_ref[:, lo:lo + FFN_TF], preferred_element_type=F32)
        up = jnp.dot(h, win_ref[:, D_FF + lo:D_FF + lo + FFN_TF], preferred_element_type=F32)
        return (_silu(gate) * up).astype(BF16)

    rows = FFN_TM // FFN_ROW_GROUPS
    xs, hbs = [], []
    for q in range(FFN_ROW_GROUPS):
        rs = slice(q * rows, (q + 1) * rows)
        xs.append(x_rows(rs))
        hbs.append(_modulated(xs[q], g_ref, mod_ref).astype(BF16))
        a_ref[rs, 0:FFN_TF] = hidden(hbs[q], 0)
    x = jnp.concatenate(xs, axis=0)
    hb = jnp.concatenate(hbs, axis=0)
    for c in range(1, D_FF // FFN_TF):
        a_ref[:, c * FFN_TF:(c + 1) * FFN_TF] = hidden(hb, c * FFN_TF)
    y = jnp.dot(a_ref[...], wout_ref[...], preferred_element_type=F32)
    out = x + (FFN_RES_WEIGHT * (1.0 + mod_ref[2:3, :])) * y
    if final_norm:
        ms = jnp.mean(out * out, axis=-1, keepdims=True)
        out = out * lax.rsqrt(ms + EPS) * fg_ref[...]
    return out


def _ffn_kernel(x_ref, mod_ref, g_ref, win_ref, wout_ref, fg_ref, o_ref, a_ref, *, final_norm):
    o_ref[...] = _ffn_body(lambda rs: x_ref[rs, :], mod_ref, g_ref, win_ref, wout_ref, fg_ref, a_ref, final_norm)


def _ffn(x, mod, layer, half, norm_g, w_in, w_out, final_g, final_norm):
    B, T, D = x.shape
    return pl.pallas_call(
        functools.partial(_ffn_kernel, final_norm=final_norm),
        grid=(B, T // FFN_TM),
        in_specs=[
            pl.BlockSpec((None, FFN_TM, D), lambda b, t: (b, t, 0)),
            _mod_spec(layer, 2 * half),
            _const_spec((1, D)),
            _const_spec((D, 2 * D_FF), (layer, half)),
            _const_spec((D_FF, D), (layer, half)),
            _const_spec((1, D)),
        ],
        out_specs=pl.BlockSpec((None, FFN_TM, D), lambda b, t: (b, t, 0)),
        out_shape=jax.ShapeDtypeStruct((B, T, D), F32),
        scratch_shapes=[pltpu.VMEM((FFN_TM, D_FF), BF16)],
        compiler_params=_params("parallel", "parallel"),
        name="ffn",
    )(x, mod, norm_g.reshape(1, D), w_in, w_out, final_g.reshape(1, D))


def _conv_kernel(x_ref, mod_ref, g_ref, wglu_ref, bglu_ref, wdw_ref, bdw_ref, lng_ref, lnb_ref,
                 wpw_ref, bpw_ref, o_ref, u_ref, c_ref):
    D = D_MODEL
    tm = CONV_TM

    @pl.when(pl.program_id(1) == 0)
    def _():
        u_ref[0:CONV_HALO, :] = jnp.zeros((CONV_HALO, D), F32)

    x = x_ref[...]
    hb = _modulated(x, g_ref, mod_ref).astype(BF16)
    a = jnp.dot(hb, wglu_ref[:, 0:D], preferred_element_type=F32) + bglu_ref[:, 0:D]
    b = jnp.dot(hb, wglu_ref[:, D:2 * D], preferred_element_type=F32) + bglu_ref[:, D:2 * D]
    u_ref[CONV_HALO:CONV_HALO + tm, :] = a * jax.nn.sigmoid(b)

    base = CONV_HALO - (CONV_WIDTH - 1)
    for rb in range(tm // CONV_RB):
        r0 = rb * CONV_RB
        for cb in range(D // CONV_CB):
            cs = slice(cb * CONV_CB, (cb + 1) * CONV_CB)
            acc = None
            for s in range(SUBLANES):
                taps = [k for k in range(CONV_WIDTH) if (base + k) % SUBLANES == s]
                rows = (base + taps[-1]) // SUBLANES * SUBLANES + CONV_RB
                slab = u_ref[r0 + s:r0 + s + rows, cs]
                part = None
                for k in taps:
                    j0 = (base + k) // SUBLANES * SUBLANES
                    term = wdw_ref[k:k + 1, cs] * slab[j0:j0 + CONV_RB]
                    part = term if part is None else part + term
                acc = part if acc is None else acc + part
            c_ref[r0:r0 + CONV_RB, cs] = acc
    u_ref[0:CONV_HALO, :] = u_ref[tm:tm + CONV_HALO, :]

    cv = c_ref[...] + bdw_ref[...]
    mu = jnp.mean(cv, axis=-1, keepdims=True)
    xc = cv - mu
    var = jnp.mean(xc * xc, axis=-1, keepdims=True)
    y = xc * lax.rsqrt(var + EPS) * lng_ref[...] + lnb_ref[...]
    out = _bdot(_silu(y), wpw_ref[...]) + bpw_ref[...]
    o_ref[...] = x + (1.0 + mod_ref[2:3, :]) * out


def _conv_module(x, mod, layer, norm_g, w_glu, b_glu, w_dw, b_dw, ln_g, ln_b, w_pw, b_pw):
    B, T, D = x.shape
    return pl.pallas_call(
        _conv_kernel,
        grid=(B, T // CONV_TM),
        in_specs=[
            pl.BlockSpec((None, CONV_TM, D), lambda b, t: (b, t, 0)),
            _mod_spec(layer, 1),
            _const_spec((1, D)),
            _const_spec((D, 2 * D)),
            _const_spec((1, 2 * D)),
            _const_spec((CONV_WIDTH, D)),
            _const_spec((1, D)),
            _const_spec((1, D)),
            _const_spec((1, D)),
            _const_spec((D, D)),
            _const_spec((1, D)),
        ],
        out_specs=pl.BlockSpec((None, CONV_TM, D), lambda b, t: (b, t, 0)),
        out_shape=jax.ShapeDtypeStruct((B, T, D), F32),
        scratch_shapes=[pltpu.VMEM((CONV_HALO + CONV_TM, D), F32), pltpu.VMEM((CONV_TM, D), F32)],
        compiler_params=_params("arbitrary", "arbitrary"),
        name="conv_module",
    )(x, mod, norm_g.reshape(1, D), w_glu, b_glu.reshape(1, 2 * D), w_dw, b_dw.reshape(1, D),
      ln_g.reshape(1, D), ln_b.reshape(1, D), w_pw, b_pw.reshape(1, D))


def _dnproj_kernel(x_ref, mod_ref, g_ref, wqkvz_ref, wab_ref, wsc_ref, alog_ref, dtb_ref,
                   q_ref, k_ref, v_ref, z_ref, gcol_ref, grow_ref, pre_ref):
    W = DN_WIDTH
    tm = DNP_TM

    @pl.when(pl.program_id(1) == 0)
    def _():
        pre_ref[0:DNP_HALO, :] = jnp.zeros((DNP_HALO, 3 * W), F32)

    hb = _modulated(x_ref[...], g_ref, mod_ref).astype(BF16)
    z_ref[...] = jnp.dot(hb, wqkvz_ref[:, 3 * W:4 * W], preferred_element_type=F32).astype(BF16)

    ab = jnp.dot(hb, wab_ref[...], preferred_element_type=F32)
    lane = lax.broadcasted_iota(jnp.int32, (tm, LANES), 1)
    g = -jnp.exp(alog_ref[...]) * jax.nn.softplus(ab + dtb_ref[...])
    g = jnp.where(lane < DN_HEADS, g, 0.0)
    ri = lax.broadcasted_iota(jnp.int32, (tm, tm), 0)
    ci = lax.broadcasted_iota(jnp.int32, (tm, tm), 1)
    tri = jnp.where((ri // CHUNK == ci // CHUNK) & (ci <= ri), 1.0, 0.0).astype(F32)
    gc = jnp.dot(tri, g, preferred_element_type=F32, precision=lax.Precision.HIGHEST)
    col = jnp.where(lane < DN_HEADS, gc, jax.nn.sigmoid(ab))
    gcol_ref[...] = col
    colt = col.T
    for c in range(tm // CHUNK):
        grow_ref[c] = colt[0:2 * DN_HEADS, c * CHUNK:(c + 1) * CHUNK]

    for j in range(3):
        pre_ref[DNP_HALO:DNP_HALO + tm, j * W:(j + 1) * W] = jnp.dot(
            hb, wqkvz_ref[:, j * W:(j + 1) * W], preferred_element_type=F32)
    base = DNP_HALO - (SHORT_CONV - 1)
    outs = (q_ref, k_ref, v_ref)
    for j in range(3):
        for h in range(DN_HEADS):
            c0 = j * W + h * DN_HEAD_DIM
            acc = jnp.zeros((tm, DN_HEAD_DIM), F32)
            for k in range(SHORT_CONV):
                acc = acc + wsc_ref[k:k + 1, c0:c0 + DN_HEAD_DIM] * pre_ref[base + k:base + k + tm, c0:c0 + DN_HEAD_DIM]
            y = _silu(acc)
            if j < 2:
                y = y * lax.rsqrt(jnp.sum(y * y, axis=-1, keepdims=True) + EPS)
                if j == 0:
                    y = y * (DN_HEAD_DIM ** -0.5)
            outs[j][:, h * DN_HEAD_DIM:(h + 1) * DN_HEAD_DIM] = y.astype(BF16)
    pre_ref[0:DNP_HALO, :] = pre_ref[tm:tm + DNP_HALO, :]


def _dn_proj(x, mod, layer, norm_g, w_qkvz, w_ab, w_sconv, a_log, dt_bias):
    B, T, D = x.shape
    W = DN_WIDTH
    tok = pl.BlockSpec((None, DNP_TM, W), lambda b, t: (b, t, 0))
    cps = DNP_TM // CHUNK
    return pl.pallas_call(
        _dnproj_kernel,
        grid=(B, T // DNP_TM),
        in_specs=[
            pl.BlockSpec((None, DNP_TM, D), lambda b, t: (b, t, 0)),
            _mod_spec(layer, 1),
            _const_spec((1, D)),
            _const_spec((D, 4 * W)),
            _const_spec((D, LANES)),
            _const_spec((SHORT_CONV, 3 * W)),
            _const_spec((1, LANES)),
            _const_spec((1, LANES)),
        ],
        out_specs=[tok, tok, tok, tok,
                   pl.BlockSpec((None, DNP_TM, LANES), lambda b, t: (b, t, 0)),
                   pl.BlockSpec((None, cps, 2 * DN_HEADS, CHUNK), lambda b, t: (b, t, 0, 0))],
        out_shape=[jax.ShapeDtypeStruct((B, T, W), BF16)] * 4 + [
            jax.ShapeDtypeStruct((B, T, LANES), F32),
            jax.ShapeDtypeStruct((B, T // CHUNK, 2 * DN_HEADS, CHUNK), F32)],
        scratch_shapes=[pltpu.VMEM((DNP_HALO + DNP_TM, 3 * W), F32)],
        compiler_params=_params("arbitrary", "arbitrary"),
        name="dn_proj",
    )(x, mod, norm_g.reshape(1, D), w_qkvz, w_ab, w_sconv, a_log, dt_bias)


def _dnchunk_kernel(q_ref, k_ref, v_ref, gcol_ref, grow_ref, o_ref, s_ref):
    C = CHUNK
    Dh = DN_HEAD_DIM
    H = DN_HEADS
    NT = (((1,), (1,)), ((), ()))
    TN = (((0,), (0,)), ((), ()))

    @pl.when(pl.program_id(1) == 0)
    def _():
        s_ref[...] = jnp.zeros(s_ref.shape, F32)

    ri = lax.broadcasted_iota(jnp.int32, (C, C), 0)
    ci = lax.broadcasted_iota(jnp.int32, (C, C), 1)
    causal = ci <= ri
    strict = ci < ri
    eye = jnp.where(ri == ci, 1.0, 0.0).astype(F32)
    same = {m: lax.shift_right_logical(ri, s) == lax.shift_right_logical(ci, s)
            for m, s in ((8, 3), (16, 4), (32, 5))}
    off = {8: same[16] & jnp.logical_not(same[8]),
           16: same[32] & jnp.logical_not(same[16]),
           32: jnp.logical_not(same[32])}

    items = [(c, h) for c in range(DNC_CPS) for h in range(H)]
    n = len(items)

    k16, q16, kf, beta, gc, eg, decay, kbeta = ([None] * n for _ in range(8))
    for i, (c, h) in enumerate(items):
        r0 = c * C
        hs = slice(h * Dh, (h + 1) * Dh)
        k16[i] = k_ref[r0:r0 + C, hs]
        q16[i] = q_ref[r0:r0 + C, hs]
        kf[i] = k16[i].astype(F32)
        gc[i] = gcol_ref[r0:r0 + C, h:h + 1]
        beta[i] = gcol_ref[r0:r0 + C, H + h:H + h + 1]
        gr = grow_ref[c, h:h + 1, :]
        decay[i] = jnp.where(causal, jnp.exp(jnp.where(causal, gc[i] - gr, 0.0)), 0.0)
        kbeta[i] = kf[i] * beta[i]
        eg[i] = jnp.exp(gc[i])

    kq = [lax.dot_general(jnp.concatenate([kbeta[i].astype(BF16), q16[i]], axis=0), k16[i], NT,
                          preferred_element_type=F32) for i in range(n)]
    a_mat = [jnp.where(strict, kq[i][0:C] * decay[i], 0.0) for i in range(n)]
    intra = [(kq[i][C:2 * C] * decay[i]).astype(BF16) for i in range(n)]

    d8 = [jnp.where(same[8], a_mat[i], 0.0) for i in range(n)]
    d8_2 = [_bdot(d8[i], d8[i]) for i in range(n)]
    d8_4 = [_bdot(d8_2[i], d8_2[i]) for i in range(n)]
    p8 = [_bdot(eye - d8[i], eye + d8_2[i]) for i in range(n)]
    tinv = [_bdot(p8[i], eye + d8_4[i]) for i in range(n)]
    for m in (8, 16, 32):
        et = [_bdot(jnp.where(off[m], a_mat[i], 0.0), tinv[i]) for i in range(n)]
        tinv = [tinv[i] - _bdot(tinv[i], et[i]) for i in range(n)]

    uw = []
    for i, (c, h) in enumerate(items):
        r0 = c * C
        vf = v_ref[r0:r0 + C, h * Dh:(h + 1) * Dh].astype(F32)
        rhs = jnp.concatenate([(vf * beta[i]).astype(BF16), (kbeta[i] * eg[i]).astype(BF16)], axis=1)
        uw.append(jnp.dot(tinv[i].astype(BF16), rhs, preferred_element_type=F32))

    state = [s_ref[h] for h in range(H)]
    for c in range(DNC_CPS):
        r0 = c * C
        idx = [c * H + h for h in range(H)]
        sb = [state[h].astype(BF16) for h in range(H)]
        wq = [jnp.concatenate([uw[i][:, Dh:2 * Dh].astype(BF16),
                               (q16[i].astype(F32) * eg[i]).astype(BF16)], axis=0) for i in idx]
        ws = [jnp.dot(wq[h], sb[h], preferred_element_type=F32) for h in range(H)]
        vnb = [(uw[idx[h]][:, 0:Dh] - ws[h][0:C]).astype(BF16) for h in range(H)]
        o2 = [jnp.dot(intra[idx[h]], vnb[h], preferred_element_type=F32) for h in range(H)]
        ds = []
        for h in range(H):
            i = idx[h]
            kd = (kf[i] * jnp.exp(gc[i][C - 1:C, :] - gc[i])).astype(BF16)
            ds.append(lax.dot_general(kd, vnb[h], TN, preferred_element_type=F32))
        for h in range(H):
            i = idx[h]
            o_ref[r0:r0 + C, h * Dh:(h + 1) * Dh] = (ws[h][C:2 * C] + o2[h]).astype(BF16)
            state[h] = state[h] * jnp.exp(gc[i][C - 1:C, :]) + ds[h]
    for h in range(H):
        s_ref[h] = state[h]


def _dn_chunk(q, k, v, gcol, grow):
    B, T, W = q.shape
    tc = DNC_CPS * CHUNK
    tok = pl.BlockSpec((None, tc, W), lambda b, t: (b, t, 0))
    return pl.pallas_call(
        _dnchunk_kernel,
        grid=(B, T // tc),
        in_specs=[tok, tok, tok,
                  pl.BlockSpec((None, tc, LANES), lambda b, t: (b, t, 0)),
                  pl.BlockSpec((None, DNC_CPS, 2 * DN_HEADS, CHUNK), lambda b, t: (b, t, 0, 0))],
        out_specs=tok,
        out_shape=jax.ShapeDtypeStruct((B, T, W), BF16),
        scratch_shapes=[pltpu.VMEM((DN_HEADS, DN_HEAD_DIM, DN_HEAD_DIM), F32)],
        compiler_params=_params("arbitrary", "arbitrary"),
        name="dn_chunk",
    )(q, k, v, gcol, grow)


def _dnout_ffn_kernel(x_ref, modm_ref, modf_ref, o_ref_in, z_ref, og_ref, wdn_ref, g_ref, win_ref, wout_ref,
                      fg_ref, out_ref, y_ref, a_ref, *, final_norm):
    Dh = DN_HEAD_DIM
    for h in range(DN_HEADS):
        hs = slice(h * Dh, (h + 1) * Dh)
        o = o_ref_in[:, hs].astype(F32)
        ms = jnp.mean(o * o, axis=-1, keepdims=True)
        y = o * lax.rsqrt(ms + EPS) * og_ref[...] * _silu(z_ref[:, hs].astype(F32))
        y_ref[:, hs] = y.astype(BF16)
    mixed = jnp.dot(y_ref[...], wdn_ref[...], preferred_element_type=F32)
    x = x_ref[...] + (1.0 + modm_ref[2:3, :]) * mixed
    out_ref[...] = _ffn_body(lambda rs: x[rs, :], modf_ref, g_ref, win_ref, wout_ref, fg_ref, a_ref, final_norm)


def _dnout_ffn(x, mod, layer, o, z, o_g, w_dn, norm_g, w_in, w_out, final_g, final_norm):
    B, T, D = x.shape
    W = DN_WIDTH
    tok = pl.BlockSpec((None, FFN_TM, W), lambda b, t: (b, t, 0))
    return pl.pallas_call(
        functools.partial(_dnout_ffn_kernel, final_norm=final_norm),
        grid=(B, T // FFN_TM),
        in_specs=[
            pl.BlockSpec((None, FFN_TM, D), lambda b, t: (b, t, 0)),
            _mod_spec(layer, 1),
            _mod_spec(layer, 2),
            tok, tok,
            _const_spec((1, DN_HEAD_DIM)),
            _const_spec((W, D)),
            _const_spec((1, D)),
            _const_spec((D, 2 * D_FF), (layer, 1)),
            _const_spec((D_FF, D), (layer, 1)),
            _const_spec((1, D)),
        ],
        out_specs=pl.BlockSpec((None, FFN_TM, D), lambda b, t: (b, t, 0)),
        out_shape=jax.ShapeDtypeStruct((B, T, D), F32),
        scratch_shapes=[pltpu.VMEM((FFN_TM, W), BF16), pltpu.VMEM((FFN_TM, D_FF), BF16)],
        compiler_params=_params("parallel", "parallel"),
        name="dnout_ffn",
    )(x, mod, mod, o, z, o_g.reshape(1, DN_HEAD_DIM), w_dn, norm_g.reshape(1, D), w_in, w_out,
      final_g.reshape(1, D))


def _lane_pad_row(v):
    return jnp.pad(v.astype(F32), (0, LANES - v.shape[0])).reshape(1, LANES)


def kernel(x, c, norm_g, w_ada, b_ada, w_ffn_in, w_ffn_out, cm_w_glu, cm_b_glu, cm_w_dw, cm_b_dw, cm_ln_g, cm_ln_b, cm_w_pw, cm_b_pw, dn_w_in, dn_w_sconv, dn_a_log, dn_dt_bias, dn_o_g, dn_w_out, final_g):
    B = x.shape[0]
    W = DN_WIDTH
    mod = _ada_mod(c, w_ada, b_ada).reshape(DEPTH, B, N_SUB, 3, D_MODEL)
    w_in, w_out = w_ffn_in.astype(BF16), w_ffn_out.astype(BF16)
    for i in range(DEPTH):
        last = i == DEPTH - 1
        x = _ffn(x, mod, i, 0, norm_g[i, 0], w_in, w_out, final_g, False)
        if i % 2 == 0:
            a = i // 2
            x = _conv_module(x, mod, i, norm_g[i, 1], cm_w_glu[a].astype(BF16), cm_b_glu[a], cm_w_dw[a],
                             cm_b_dw[a], cm_ln_g[a], cm_ln_b[a], cm_w_pw[a].astype(BF16), cm_b_pw[a])
            x = _ffn(x, mod, i, 1, norm_g[i, 2], w_in, w_out, final_g, last)
        else:
            m = i // 2
            w_dn = dn_w_in[m]
            w_ab = jnp.pad(w_dn[:, 4 * W:], ((0, 0), (0, LANES - 2 * DN_HEADS))).astype(BF16)
            q, k, v, z, gcol, grow = _dn_proj(
                x, mod, i, norm_g[i, 1], w_dn[:, :4 * W].astype(BF16), w_ab, dn_w_sconv[m],
                _lane_pad_row(dn_a_log[m]), _lane_pad_row(dn_dt_bias[m]))
            o = _dn_chunk(q, k, v, gcol, grow)
            x = _dnout_ffn(x, mod, i, o, z, dn_o_g[m], dn_w_out[m].astype(BF16), norm_g[i, 2], w_in, w_out,
                           final_g, last)
    return x
```

```python
import functools

import jax
import jax.numpy as jnp
from jax import lax
from jax.experimental import pallas as pl
from jax.experimental.pallas import tpu as pltpu

D_MODEL = 1024
DEPTH = 4
N_SUB = 3
D_FF = 2816
CONV_WIDTH = 31
DN_HEADS = 8
DN_HEAD_DIM = 128
DN_WIDTH = DN_HEADS * DN_HEAD_DIM
SHORT_CONV = 4
CHUNK = 64
EPS = 1e-6
FFN_RES_WEIGHT = 0.5

LANES = 128
SUBLANES = 8
VMEM_LIMIT_BYTES = 56 * 1024 * 1024

FFN_TM = 1024
FFN_TF = 256
FFN_ROW_GROUPS = 4
CONV_TM = 256
CONV_HALO = 32
CONV_RB = 64
CONV_CB = 128
DNP_TM = 256
DNP_HALO = 8
DNC_CPS = 8
MOD_TN = 1024

F32 = jnp.float32
BF16 = jnp.bfloat16


def _params(*sem):
    return pltpu.CompilerParams(dimension_semantics=sem, vmem_limit_bytes=VMEM_LIMIT_BYTES)


def _const_spec(shape, lead=()):
    index = tuple(lead) + (0,) * len(shape)
    return pl.BlockSpec((None,) * len(lead) + tuple(shape), lambda *_: index, pipeline_mode=pl.Buffered(1))


def _mod_spec(layer, sub):
    return pl.BlockSpec((None, None, None, 3, D_MODEL), lambda b, t: (layer, b, sub, 0, 0))


def _bdot(a, b):
    return jnp.dot(a.astype(BF16), b.astype(BF16), preferred_element_type=F32)


def _silu(x):
    return x * jax.nn.sigmoid(x)


def _modulated(x, g_ref, mod_ref):
    ms = jnp.mean(x * x, axis=-1, keepdims=True)
    y = x * lax.rsqrt(ms + EPS) * g_ref[...]
    return y * (1.0 + mod_ref[1:2, :]) + mod_ref[0:1, :]


def _mod_kernel(c_ref, w_ref, b_ref, o_ref):
    cs = _silu(c_ref[...])
    o_ref[...] = _bdot(cs, w_ref[...]) + b_ref[...]


def _ada_mod(c, w_ada, b_ada):
    B = c.shape[0]
    n = w_ada.shape[-1]
    return pl.pallas_call(
        _mod_kernel,
        grid=(DEPTH, n // MOD_TN),
        in_specs=[
            pl.BlockSpec((B, D_MODEL), lambda i, j: (0, 0)),
            pl.BlockSpec((None, D_MODEL, MOD_TN), lambda i, j: (i, 0, j)),
            pl.BlockSpec((None, 1, MOD_TN), lambda i, j: (i, 0, j)),
        ],
        out_specs=pl.BlockSpec((None, B, MOD_TN), lambda i, j: (i, 0, j)),
        out_shape=jax.ShapeDtypeStruct((DEPTH, B, n), F32),
        compiler_params=_params("parallel", "parallel"),
        name="ada_mod",
    )(c, w_ada, b_ada.reshape(DEPTH, 1, n))


def _ffn_body(x_rows, mod_ref, g_ref, win_ref, wout_ref, fg_ref, a_ref, final_norm):
    def hidden(h, lo):
        gate = jnp.dot(h, win_ref[:, lo:lo + FFN_TF], preferred_element_type=F32)
        up = jnp.dot(h, win_ref[:, D_FF + lo:D_FF + lo + FFN_TF], preferred_element_type=F32)
        return (_silu(gate) * up).astype(BF16)

    rows = FFN_TM // FFN_ROW_GROUPS
    xs, hbs = [], []
    for q in range(FFN_ROW_GROUPS):
        rs = slice(q * rows, (q + 1) * rows)
        xs.append(x_rows(rs))
        hbs.append(_modulated(xs[q], g_ref, mod_ref).astype(BF16))
        a_ref[rs, 0:FFN_TF] = hidden(hbs[q], 0)
    x = jnp.concatenate(xs, axis=0)
    hb = jnp.concatenate(hbs, axis=0)
    for c in range(1, D_FF // FFN_TF):
        a_ref[:, c * FFN_TF:(c + 1) * FFN_TF] = hidden(hb, c * FFN_TF)
    y = jnp.dot(a_ref[...], wout_ref[...], preferred_element_type=F32)
    out = x + (FFN_RES_WEIGHT * (1.0 + mod_ref[2:3, :])) * y
    if final_norm:
        ms = jnp.mean(out * out, axis=-1, keepdims=True)
        out = out * lax.rsqrt(ms + EPS) * fg_ref[...]
    return out


def _ffn_kernel(x_ref, mod_ref, g_ref, win_ref, wout_ref, fg_ref, o_ref, a_ref, *, final_norm):
    o_ref[...] = _ffn_body(lambda rs: x_ref[rs, :], mod_ref, g_ref, win_ref, wout_ref, fg_ref, a_ref, final_norm)


def _ffn(x, mod, layer, half, norm_g, w_in, w_out, final_g, final_norm):
    B, T, D = x.shape
    return pl.pallas_call(
        functools.partial(_ffn_kernel, final_norm=final_norm),
        grid=(B, T // FFN_TM),
        in_specs=[
            pl.BlockSpec((None, FFN_TM, D), lambda b, t: (b, t, 0)),
            _mod_spec(layer, 2 * half),
            _const_spec((1, D)),
            _const_spec((D, 2 * D_FF), (layer, half)),
            _const_spec((D_FF, D), (layer, half)),
            _const_spec((1, D)),
        ],
        out_specs=pl.BlockSpec((None, FFN_TM, D), lambda b, t: (b, t, 0)),
        out_shape=jax.ShapeDtypeStruct((B, T, D), F32),
        scratch_shapes=[pltpu.VMEM((FFN_TM, D_FF), BF16)],
        compiler_params=_params("parallel", "parallel"),
        name="ffn",
    )(x, mod, norm_g.reshape(1, D), w_in, w_out, final_g.reshape(1, D))


def _conv_kernel(x_ref, mod_ref, g_ref, wglu_ref, bglu_ref, wdw_ref, bdw_ref, lng_ref, lnb_ref,
                 wpw_ref, bpw_ref, o_ref, u_ref, c_ref):
    D = D_MODEL
    tm = CONV_TM

    @pl.when(pl.program_id(1) == 0)
    def _():
        u_ref[0:CONV_HALO, :] = jnp.zeros((CONV_HALO, D), F32)

    x = x_ref[...]
    hb = _modulated(x, g_ref, mod_ref).astype(BF16)
    a = jnp.dot(hb, wglu_ref[:, 0:D], preferred_element_type=F32) + bglu_ref[:, 0:D]
    b = jnp.dot(hb, wglu_ref[:, D:2 * D], preferred_element_type=F32) + bglu_ref[:, D:2 * D]
    u_ref[CONV_HALO:CONV_HALO + tm, :] = a * jax.nn.sigmoid(b)

    base = CONV_HALO - (CONV_WIDTH - 1)
    for rb in range(tm // CONV_RB):
        r0 = rb * CONV_RB
        for cb in range(D // CONV_CB):
            cs = slice(cb * CONV_CB, (cb + 1) * CONV_CB)
            acc = None
            for s in range(SUBLANES):
                taps = [k for k in range(CONV_WIDTH) if (base + k) % SUBLANES == s]
                rows = (base + taps[-1]) // SUBLANES * SUBLANES + CONV_RB
                slab = u_ref[r0 + s:r0 + s + rows, cs]
                part = None
                for k in taps:
                    j0 = (base + k) // SUBLANES * SUBLANES
                    term = wdw_ref[k:k + 1, cs] * slab[j0:j0 + CONV_RB]
                    part = term if part is None else part + term
                acc = part if acc is None else acc + part
            c_ref[r0:r0 + CONV_RB, cs] = acc
    u_ref[0:CONV_HALO, :] = u_ref[tm:tm + CONV_HALO, :]

    cv = c_ref[...] + bdw_ref[...]
    mu = jnp.mean(cv, axis=-1, keepdims=True)
    xc = cv - mu
    var = jnp.mean(xc * xc, axis=-1, keepdims=True)
    y = xc * lax.rsqrt(var + EPS) * lng_ref[...] + lnb_ref[...]
    out = _bdot(_silu(y), wpw_ref[...]) + bpw_ref[...]
    o_ref[...] = x + (1.0 + mod_ref[2:3, :]) * out


def _conv_module(x, mod, layer, norm_g, w_glu, b_glu, w_dw, b_dw, ln_g, ln_b, w_pw, b_pw):
    B, T, D = x.shape
    return pl.pallas_call(
        _conv_kernel,
        grid=(B, T // CONV_TM),
        in_specs=[
            pl.BlockSpec((None, CONV_TM, D), lambda b, t: (b, t, 0)),
            _mod_spec(layer, 1),
            _const_spec((1, D)),
            _const_spec((D, 2 * D)),
            _const_spec((1, 2 * D)),
            _const_spec((CONV_WIDTH, D)),
            _const_spec((1, D)),
            _const_spec((1, D)),
            _const_spec((1, D)),
            _const_spec((D, D)),
            _const_spec((1, D)),
        ],
        out_specs=pl.BlockSpec((None, CONV_TM, D), lambda b, t: (b, t, 0)),
        out_shape=jax.ShapeDtypeStruct((B, T, D), F32),
        scratch_shapes=[pltpu.VMEM((CONV_HALO + CONV_TM, D), F32), pltpu.VMEM((CONV_TM, D), F32)],
        compiler_params=_params("arbitrary", "arbitrary"),
        name="conv_module",
    )(x, mod, norm_g.reshape(1, D), w_glu, b_glu.reshape(1, 2 * D), w_dw, b_dw.reshape(1, D),
      ln_g.reshape(1, D), ln_b.reshape(1, D), w_pw, b_pw.reshape(1, D))


def _dnproj_kernel(x_ref, mod_ref, g_ref, wqkvz_ref, wab_ref, wsc_ref, alog_ref, dtb_ref,
                   q_ref, k_ref, v_ref, z_ref, gcol_ref, grow_ref, pre_ref):
    W = DN_WIDTH
    tm = DNP_TM

    @pl.when(pl.program_id(1) == 0)
    def _():
        pre_ref[0:DNP_HALO, :] = jnp.zeros((DNP_HALO, 3 * W), F32)

    hb = _modulated(x_ref[...], g_ref, mod_ref).astype(BF16)
    z_ref[...] = jnp.dot(hb, wqkvz_ref[:, 3 * W:4 * W], preferred_element_type=F32).astype(BF16)

    ab = jnp.dot(hb, wab_ref[...], preferred_element_type=F32)
    lane = lax.broadcasted_iota(jnp.int32, (tm, LANES), 1)
    g = -jnp.exp(alog_ref[...]) * jax.nn.softplus(ab + dtb_ref[...])
    g = jnp.where(lane < DN_HEADS, g, 0.0)
    ri = lax.broadcasted_iota(jnp.int32, (tm, tm), 0)
    ci = lax.broadcasted_iota(jnp.int32, (tm, tm), 1)
    tri = jnp.where((ri // CHUNK == ci // CHUNK) & (ci <= ri), 1.0, 0.0).astype(F32)
    gc = jnp.dot(tri, g, preferred_element_type=F32, precision=lax.Precision.HIGHEST)
    col = jnp.where(lane < DN_HEADS, gc, jax.nn.sigmoid(ab))
    gcol_ref[...] = col
    colt = col.T
    grow_ref[...] = jnp.zeros(grow_ref.shape, F32)
    for c in range(tm // CHUNK):
        grow_ref[:, c * LANES:c * LANES + CHUNK] = colt[0:2 * DN_HEADS, c * CHUNK:(c + 1) * CHUNK]

    for j in range(3):
        pre_ref[DNP_HALO:DNP_HALO + tm, j * W:(j + 1) * W] = jnp.dot(
            hb, wqkvz_ref[:, j * W:(j + 1) * W], preferred_element_type=F32)
    base = DNP_HALO - (SHORT_CONV - 1)
    outs = (q_ref, k_ref, v_ref)
    for j in range(3):
        for h in range(DN_HEADS):
            c0 = j * W + h * DN_HEAD_DIM
            acc = jnp.zeros((tm, DN_HEAD_DIM), F32)
            for k in range(SHORT_CONV):
                acc = acc + wsc_ref[k:k + 1, c0:c0 + DN_HEAD_DIM] * pre_ref[base + k:base + k + tm, c0:c0 + DN_HEAD_DIM]
            y = _silu(acc)
            if j < 2:
                y = y * lax.rsqrt(jnp.sum(y * y, axis=-1, keepdims=True) + EPS)
                if j == 0:
                    y = y * (DN_HEAD_DIM ** -0.5)
            outs[j][:, h * DN_HEAD_DIM:(h + 1) * DN_HEAD_DIM] = y.astype(BF16)
    pre_ref[0:DNP_HALO, :] = pre_ref[tm:tm + DNP_HALO, :]


def _dn_proj(x, mod, layer, norm_g, w_qkvz, w_ab, w_sconv, a_log, dt_bias):
    B, T, D = x.shape
    W = DN_WIDTH
    tok = pl.BlockSpec((None, DNP_TM, W), lambda b, t: (b, t, 0))
    return pl.pallas_call(
        _dnproj_kernel,
        grid=(B, T // DNP_TM),
        in_specs=[
            pl.BlockSpec((None, DNP_TM, D), lambda b, t: (b, t, 0)),
            _mod_spec(layer, 1),
            _const_spec((1, D)),
            _const_spec((D, 4 * W)),
            _const_spec((D, LANES)),
            _const_spec((SHORT_CONV, 3 * W)),
            _const_spec((1, LANES)),
            _const_spec((1, LANES)),
        ],
        out_specs=[tok, tok, tok, tok,
                   pl.BlockSpec((None, DNP_TM, LANES), lambda b, t: (b, t, 0)),
                   pl.BlockSpec((None, 2 * DN_HEADS, DNP_TM // CHUNK * LANES), lambda b, t: (b, 0, t))],
        out_shape=[jax.ShapeDtypeStruct((B, T, W), BF16)] * 4 + [
            jax.ShapeDtypeStruct((B, T, LANES), F32),
            jax.ShapeDtypeStruct((B, 2 * DN_HEADS, T // CHUNK * LANES), F32)],
        scratch_shapes=[pltpu.VMEM((DNP_HALO + DNP_TM, 3 * W), F32)],
        compiler_params=_params("arbitrary", "arbitrary"),
        name="dn_proj",
    )(x, mod, norm_g.reshape(1, D), w_qkvz, w_ab, w_sconv, a_log, dt_bias)


def _dnchunk_kernel(q_ref, k_ref, v_ref, gcol_ref, grow_ref, o_ref, s_ref):
    C = CHUNK
    Dh = DN_HEAD_DIM
    H = DN_HEADS
    NT = (((1,), (1,)), ((), ()))
    TN = (((0,), (0,)), ((), ()))

    @pl.when(pl.program_id(1) == 0)
    def _():
        s_ref[...] = jnp.zeros(s_ref.shape, F32)

    ri = lax.broadcasted_iota(jnp.int32, (C, C), 0)
    ci = lax.broadcasted_iota(jnp.int32, (C, C), 1)
    causal = ci <= ri
    strict = ci < ri
    eye = jnp.where(ri == ci, 1.0, 0.0).astype(F32)
    same = {m: lax.shift_right_logical(ri, s) == lax.shift_right_logical(ci, s)
            for m, s in ((8, 3), (16, 4), (32, 5))}
    off = {8: same[16] & jnp.logical_not(same[8]),
           16: same[32] & jnp.logical_not(same[16]),
           32: jnp.logical_not(same[32])}

    items = [(c, h) for c in range(DNC_CPS) for h in range(H)]
    n = len(items)

    k16, q16, kf, beta, gc, eg, decay, kbeta = ([None] * n for _ in range(8))
    for i, (c, h) in enumerate(items):
        r0 = c * C
        hs = slice(h * Dh, (h + 1) * Dh)
        k16[i] = k_ref[r0:r0 + C, hs]
        q16[i] = q_ref[r0:r0 + C, hs]
        kf[i] = k16[i].astype(F32)
        gc[i] = gcol_ref[r0:r0 + C, h:h + 1]
        beta[i] = gcol_ref[r0:r0 + C, H + h:H + h + 1]
        gr = grow_ref[h:h + 1, c * LANES:c * LANES + C]
        decay[i] = jnp.where(causal, jnp.exp(jnp.where(causal, gc[i] - gr, 0.0)), 0.0)
        kbeta[i] = kf[i] * beta[i]
        eg[i] = jnp.exp(gc[i])

    kq = [lax.dot_general(jnp.concatenate([kbeta[i].astype(BF16), q16[i]], axis=0), k16[i], NT,
                          preferred_element_type=F32) for i in range(n)]
    a_mat = [jnp.where(strict, kq[i][0:C] * decay[i], 0.0) for i in range(n)]
    intra = [(kq[i][C:2 * C] * decay[i]).astype(BF16) for i in range(n)]

    d8 = [jnp.where(same[8], a_mat[i], 0.0) for i in range(n)]
    d8_2 = [_bdot(d8[i], d8[i]) for i in range(n)]
    d8_4 = [_bdot(d8_2[i], d8_2[i]) for i in range(n)]
    p8 = [_bdot(eye - d8[i], eye + d8_2[i]) for i in range(n)]
    tinv = [_bdot(p8[i], eye + d8_4[i]) for i in range(n)]
    for m in (8, 16, 32):
        et = [_bdot(jnp.where(off[m], a_mat[i], 0.0), tinv[i]) for i in range(n)]
        tinv = [tinv[i] - _bdot(tinv[i], et[i]) for i in range(n)]

    uw = []
    for i, (c, h) in enumerate(items):
        r0 = c * C
        vf = v_ref[r0:r0 + C, h * Dh:(h + 1) * Dh].astype(F32)
        rhs = jnp.concatenate([(vf * beta[i]).astype(BF16), (kbeta[i] * eg[i]).astype(BF16)], axis=1)
        uw.append(jnp.dot(tinv[i].astype(BF16), rhs, preferred_element_type=F32))

    state = [s_ref[h] for h in range(H)]
    for c in range(DNC_CPS):
        r0 = c * C
        idx = [c * H + h for h in range(H)]
        sb = [state[h].astype(BF16) for h in range(H)]
        wq = [jnp.concatenate([uw[i][:, Dh:2 * Dh].astype(BF16),
                               (q16[i].astype(F32) * eg[i]).astype(BF16)], axis=0) for i in idx]
        ws = [jnp.dot(wq[h], sb[h], preferred_element_type=F32) for h in range(H)]
        vnb = [(uw[idx[h]][:, 0:Dh] - ws[h][0:C]).astype(BF16) for h in range(H)]
        o2 = [jnp.dot(intra[idx[h]], vnb[h], preferred_element_type=F32) for h in range(H)]
        ds = []
        for h in range(H):
            i = idx[h]
            kd = (kf[i] * jnp.exp(gc[i][C - 1:C, :] - gc[i])).astype(BF16)
            ds.append(lax.dot_general(kd, vnb[h], TN, preferred_element_type=F32))
        for h in range(H):
            i = idx[h]
            o_ref[r0:r0 + C, h * Dh:(h + 1) * Dh] = (ws[h][C:2 * C] + o2[h]).astype(BF16)
            state[h] = state[h] * jnp.exp(gc[i][C - 1:C, :]) + ds[h]
    for h in range(H):
        s_ref[h] = state[h]


def _dn_chunk(q, k, v, gcol, grow):
    B, T, W = q.shape
    tc = DNC_CPS * CHUNK
    tok = pl.BlockSpec((None, tc, W), lambda b, t: (b, t, 0))
    return pl.pallas_call(
        _dnchunk_kernel,
        grid=(B, T // tc),
        in_specs=[tok, tok, tok,
                  pl.BlockSpec((None, tc, LANES), lambda b, t: (b, t, 0)),
                  pl.BlockSpec((None, 2 * DN_HEADS, DNC_CPS * LANES), lambda b, t: (b, 0, t))],
        out_specs=tok,
        out_shape=jax.ShapeDtypeStruct((B, T, W), BF16),
        scratch_shapes=[pltpu.VMEM((DN_HEADS, DN_HEAD_DIM, DN_HEAD_DIM), F32)],
        compiler_params=_params("arbitrary", "arbitrary"),
        name="dn_chunk",
    )(q, k, v, gcol, grow)


def _dnout_ffn_kernel(x_ref, modm_ref, modf_ref, o_ref_in, z_ref, og_ref, wdn_ref, g_ref, win_ref, wout_ref,
                      fg_ref, out_ref, y_ref, a_ref, *, final_norm):
    Dh = DN_HEAD_DIM
    for h in range(DN_HEADS):
        hs = slice(h * Dh, (h + 1) * Dh)
        o = o_ref_in[:, hs].astype(F32)
        ms = jnp.mean(o * o, axis=-1, keepdims=True)
        y = o * lax.rsqrt(ms + EPS) * og_ref[...] * _silu(z_ref[:, hs].astype(F32))
        y_ref[:, hs] = y.astype(BF16)
    mixed = jnp.dot(y_ref[...], wdn_ref[...], preferred_element_type=F32)
    x = x_ref[...] + (1.0 + modm_ref[2:3, :]) * mixed
    out_ref[...] = _ffn_body(lambda rs: x[rs, :], modf_ref, g_ref, win_ref, wout_ref, fg_ref, a_ref, final_norm)


def _dnout_ffn(x, mod, layer, o, z, o_g, w_dn, norm_g, w_in, w_out, final_g, final_norm):
    B, T, D = x.shape
    W = DN_WIDTH
    tok = pl.BlockSpec((None, FFN_TM, W), lambda b, t: (b, t, 0))
    return pl.pallas_call(
        functools.partial(_dnout_ffn_kernel, final_norm=final_norm),
        grid=(B, T // FFN_TM),
        in_specs=[
            pl.BlockSpec((None, FFN_TM, D), lambda b, t: (b, t, 0)),
            _mod_spec(layer, 1),
            _mod_spec(layer, 2),
            tok, tok,
            _const_spec((1, DN_HEAD_DIM)),
            _const_spec((W, D)),
            _const_spec((1, D)),
            _const_spec((D, 2 * D_FF), (layer, 1)),
            _const_spec((D_FF, D), (layer, 1)),
            _const_spec((1, D)),
        ],
        out_specs=pl.BlockSpec((None, FFN_TM, D), lambda b, t: (b, t, 0)),
        out_shape=jax.ShapeDtypeStruct((B, T, D), F32),
        scratch_shapes=[pltpu.VMEM((FFN_TM, W), BF16), pltpu.VMEM((FFN_TM, D_FF), BF16)],
        compiler_params=_params("parallel", "parallel"),
        name="dnout_ffn",
    )(x, mod, mod, o, z, o_g.reshape(1, DN_HEAD_DIM), w_dn, norm_g.reshape(1, D), w_in, w_out,
      final_g.reshape(1, D))


def _lane_pad_row(v):
    return jnp.pad(v.astype(F32), (0, LANES - v.shape[0])).reshape(1, LANES)


def kernel(x, c, norm_g, w_ada, b_ada, w_ffn_in, w_ffn_out, cm_w_glu, cm_b_glu, cm_w_dw, cm_b_dw, cm_ln_g, cm_ln_b, cm_w_pw, cm_b_pw, dn_w_in, dn_w_sconv, dn_a_log, dn_dt_bias, dn_o_g, dn_w_out, final_g):
    B = x.shape[0]
    W = DN_WIDTH
    mod = _ada_mod(c, w_ada, b_ada).reshape(DEPTH, B, N_SUB, 3, D_MODEL)
    w_in, w_out = w_ffn_in.astype(BF16), w_ffn_out.astype(BF16)
    for i in range(DEPTH):
        last = i == DEPTH - 1
        x = _ffn(x, mod, i, 0, norm_g[i, 0], w_in, w_out, final_g, False)
        if i % 2 == 0:
            a = i // 2
            x = _conv_module(x, mod, i, norm_g[i, 1], cm_w_glu[a].astype(BF16), cm_b_glu[a], cm_w_dw[a],
                             cm_b_dw[a], cm_ln_g[a], cm_ln_b[a], cm_w_pw[a].astype(BF16), cm_b_pw[a])
            x = _ffn(x, mod, i, 1, norm_g[i, 2], w_in, w_out, final_g, last)
        else:
            m = i // 2
            w_dn = dn_w_in[m]
            w_ab = jnp.pad(w_dn[:, 4 * W:], ((0, 0), (0, LANES - 2 * DN_HEADS))).astype(BF16)
            q, k, v, z, gcol, grow = _dn_proj(
                x, mod, i, norm_g[i, 1], w_dn[:, :4 * W].astype(BF16), w_ab, dn_w_sconv[m],
                _lane_pad_row(dn_a_log[m]), _lane_pad_row(dn_dt_bias[m]))
            o = _dn_chunk(q, k, v, gcol, grow)
            x = _dnout_ffn(x, mod, i, o, z, dn_o_g[m], dn_w_out[m].astype(BF16), norm_g[i, 2], w_in, w_out,
                           final_g, last)
    return x
```

```python
import functools

import jax
import jax.numpy as jnp
from jax import lax
from jax.experimental import pallas as pl
from jax.experimental.pallas import tpu as pltpu

D_MODEL = 1024
DEPTH = 4
N_SUB = 3
D_FF = 2816
CONV_WIDTH = 31
DN_HEADS = 8
DN_HEAD_DIM = 128
DN_WIDTH = DN_HEADS * DN_HEAD_DIM
SHORT_CONV = 4
CHUNK = 64
EPS = 1e-6
FFN_RES_WEIGHT = 0.5

LANES = 128
SUBLANES = 8
VMEM_LIMIT_BYTES = 56 * 1024 * 1024

FFN_TM = 1024
FFN_TF = 256
FFN_ROW_GROUPS = 4
CONV_TM = 512
CONV_HALO = 32
CONV_RB = 64
CONV_CB = 128
DNP_TM = 256
DNP_HALO = 8
DNC_CPS = 8
MOD_TN = 2304

F32 = jnp.float32
BF16 = jnp.bfloat16


def _params(*sem):
    return pltpu.CompilerParams(dimension_semantics=sem, vmem_limit_bytes=VMEM_LIMIT_BYTES)


def _const_spec(shape, lead=()):
    index = tuple(lead) + (0,) * len(shape)
    return pl.BlockSpec((None,) * len(lead) + tuple(shape), lambda *_: index, pipeline_mode=pl.Buffered(1))


def _mod_spec(layer, sub):
    return pl.BlockSpec((None, None, None, 3, D_MODEL), lambda b, t: (layer, b, sub, 0, 0))


def _bdot(a, b):
    return jnp.dot(a.astype(BF16), b.astype(BF16), preferred_element_type=F32)


def _silu(x):
    return x * jax.nn.sigmoid(x)


def _modulated(x, g_ref, mod_ref):
    ms = jnp.mean(x * x, axis=-1, keepdims=True)
    y = x * lax.rsqrt(ms + EPS) * g_ref[...]
    return y * (1.0 + mod_ref[1:2, :]) + mod_ref[0:1, :]


def _mod_kernel(c_ref, w_ref, b_ref, o_ref):
    cs = _silu(c_ref[...])
    o_ref[...] = _bdot(cs, w_ref[...]) + b_ref[...]


def _ada_mod(c, w_ada, b_ada):
    B = c.shape[0]
    n = w_ada.shape[-1]
    return pl.pallas_call(
        _mod_kernel,
        grid=(DEPTH, n // MOD_TN),
        in_specs=[
            pl.BlockSpec((B, D_MODEL), lambda i, j: (0, 0)),
            pl.BlockSpec((None, D_MODEL, MOD_TN), lambda i, j: (i, 0, j)),
            pl.BlockSpec((None, 1, MOD_TN), lambda i, j: (i, 0, j)),
        ],
        out_specs=pl.BlockSpec((None, B, MOD_TN), lambda i, j: (i, 0, j)),
        out_shape=jax.ShapeDtypeStruct((DEPTH, B, n), F32),
        compiler_params=_params("parallel", "parallel"),
        name="ada_mod",
    )(c, w_ada, b_ada.reshape(DEPTH, 1, n))


def _ffn_body(x_rows, mod_ref, g_ref, win_ref, wout_ref, fg_ref, a_ref, final_norm):
    def hidden(h, lo):
        gate = jnp.dot(h, win_ref[:, lo:lo + FFN_TF], preferred_element_type=F32)
        up = jnp.dot(h, win_ref[:, D_FF + lo:D_FF + lo + FFN_TF], preferred_element_type=F32)
        return (_silu(gate) * up).astype(BF16)

    rows = FFN_TM // FFN_ROW_GROUPS
    xs, hbs = [], []
    for q in range(FFN_ROW_GROUPS):
        rs = slice(q * rows, (q + 1) * rows)
        xs.append(x_rows(rs))
        hbs.append(_modulated(xs[q], g_ref, mod_ref).astype(BF16))
        a_ref[rs, 0:FFN_TF] = hidden(hbs[q], 0)
    x = jnp.concatenate(xs, axis=0)
    hb = jnp.concatenate(hbs, axis=0)
    for c in range(1, D_FF // FFN_TF):
        a_ref[:, c * FFN_TF:(c + 1) * FFN_TF] = hidden(hb, c * FFN_TF)
    y = jnp.dot(a_ref[...], wout_ref[...], preferred_element_type=F32)
    out = x + (FFN_RES_WEIGHT * (1.0 + mod_ref[2:3, :])) * y
    if final_norm:
        ms = jnp.mean(out * out, axis=-1, keepdims=True)
        out = out * lax.rsqrt(ms + EPS) * fg_ref[...]
    return out


def _ffn_kernel(x_ref, mod_ref, g_ref, win_ref, wout_ref, fg_ref, o_ref, a_ref, *, final_norm):
    o_ref[...] = _ffn_body(lambda rs: x_ref[rs, :], mod_ref, g_ref, win_ref, wout_ref, fg_ref, a_ref, final_norm)


def _ffn(x, mod, layer, half, norm_g, w_in, w_out, final_g, final_norm):
    B, T, D = x.shape
    return pl.pallas_call(
        functools.partial(_ffn_kernel, final_norm=final_norm),
        grid=(B, T // FFN_TM),
        in_specs=[
            pl.BlockSpec((None, FFN_TM, D), lambda b, t: (b, t, 0)),
            _mod_spec(layer, 2 * half),
            _const_spec((1, D)),
            _const_spec((D, 2 * D_FF), (layer, half)),
            _const_spec((D_FF, D), (layer, half)),
            _const_spec((1, D)),
        ],
        out_specs=pl.BlockSpec((None, FFN_TM, D), lambda b, t: (b, t, 0)),
        out_shape=jax.ShapeDtypeStruct((B, T, D), F32),
        scratch_shapes=[pltpu.VMEM((FFN_TM, D_FF), BF16)],
        compiler_params=_params("parallel", "parallel"),
        name="ffn",
    )(x, mod, norm_g.reshape(1, D), w_in, w_out, final_g.reshape(1, D))


def _conv_kernel(x_ref, mod_ref, g_ref, wglu_ref, bglu_ref, wdw_ref, bdw_ref, lng_ref, lnb_ref,
                 wpw_ref, bpw_ref, o_ref, u_ref, c_ref):
    D = D_MODEL
    tm = CONV_TM

    @pl.when(pl.program_id(1) == 0)
    def _():
        u_ref[0:CONV_HALO, :] = jnp.zeros((CONV_HALO, D), F32)

    x = x_ref[...]
    hb = _modulated(x, g_ref, mod_ref).astype(BF16)
    a = jnp.dot(hb, wglu_ref[:, 0:D], preferred_element_type=F32) + bglu_ref[:, 0:D]
    b = jnp.dot(hb, wglu_ref[:, D:2 * D], preferred_element_type=F32) + bglu_ref[:, D:2 * D]
    u_ref[CONV_HALO:CONV_HALO + tm, :] = a * jax.nn.sigmoid(b)

    base = CONV_HALO - (CONV_WIDTH - 1)
    for rb in range(tm // CONV_RB):
        r0 = rb * CONV_RB
        for cb in range(D // CONV_CB):
            cs = slice(cb * CONV_CB, (cb + 1) * CONV_CB)
            acc = None
            for s in range(SUBLANES):
                taps = [k for k in range(CONV_WIDTH) if (base + k) % SUBLANES == s]
                rows = (base + taps[-1]) // SUBLANES * SUBLANES + CONV_RB
                slab = u_ref[r0 + s:r0 + s + rows, cs]
                part = None
                for k in taps:
                    j0 = (base + k) // SUBLANES * SUBLANES
                    term = wdw_ref[k:k + 1, cs] * slab[j0:j0 + CONV_RB]
                    part = term if part is None else part + term
                acc = part if acc is None else acc + part
            c_ref[r0:r0 + CONV_RB, cs] = acc
    u_ref[0:CONV_HALO, :] = u_ref[tm:tm + CONV_HALO, :]

    cv = c_ref[...] + bdw_ref[...]
    mu = jnp.mean(cv, axis=-1, keepdims=True)
    xc = cv - mu
    var = jnp.mean(xc * xc, axis=-1, keepdims=True)
    y = xc * lax.rsqrt(var + EPS) * lng_ref[...] + lnb_ref[...]
    out = _bdot(_silu(y), wpw_ref[...]) + bpw_ref[...]
    o_ref[...] = x + (1.0 + mod_ref[2:3, :]) * out


def _conv_module(x, mod, layer, norm_g, w_glu, b_glu, w_dw, b_dw, ln_g, ln_b, w_pw, b_pw):
    B, T, D = x.shape
    return pl.pallas_call(
        _conv_kernel,
        grid=(B, T // CONV_TM),
        in_specs=[
            pl.BlockSpec((None, CONV_TM, D), lambda b, t: (b, t, 0)),
            _mod_spec(layer, 1),
            _const_spec((1, D)),
            _const_spec((D, 2 * D)),
            _const_spec((1, 2 * D)),
            _const_spec((CONV_WIDTH, D)),
            _const_spec((1, D)),
            _const_spec((1, D)),
            _const_spec((1, D)),
            _const_spec((D, D)),
            _const_spec((1, D)),
        ],
        out_specs=pl.BlockSpec((None, CONV_TM, D), lambda b, t: (b, t, 0)),
        out_shape=jax.ShapeDtypeStruct((B, T, D), F32),
        scratch_shapes=[pltpu.VMEM((CONV_HALO + CONV_TM, D), F32), pltpu.VMEM((CONV_TM, D), F32)],
        compiler_params=_params("arbitrary", "arbitrary"),
        name="conv_module",
    )(x, mod, norm_g.reshape(1, D), w_glu, b_glu.reshape(1, 2 * D), w_dw, b_dw.reshape(1, D),
      ln_g.reshape(1, D), ln_b.reshape(1, D), w_pw, b_pw.reshape(1, D))


def _dnproj_kernel(x_ref, mod_ref, g_ref, wqkvz_ref, wab_ref, wsc_ref, alog_ref, dtb_ref,
                   q_ref, k_ref, v_ref, z_ref, gcol_ref, grow_ref, pre_ref):
    W = DN_WIDTH
    tm = DNP_TM

    @pl.when(pl.program_id(1) == 0)
    def _():
        pre_ref[0:DNP_HALO, :] = jnp.zeros((DNP_HALO, 3 * W), F32)

    hb = _modulated(x_ref[...], g_ref, mod_ref).astype(BF16)
    z_ref[...] = jnp.dot(hb, wqkvz_ref[:, 3 * W:4 * W], preferred_element_type=F32).astype(BF16)

    ab = jnp.dot(hb, wab_ref[...], preferred_element_type=F32)
    lane = lax.broadcasted_iota(jnp.int32, (tm, LANES), 1)
    g = -jnp.exp(alog_ref[...]) * jax.nn.softplus(ab + dtb_ref[...])
    g = jnp.where(lane < DN_HEADS, g, 0.0)
    ri = lax.broadcasted_iota(jnp.int32, (tm, tm), 0)
    ci = lax.broadcasted_iota(jnp.int32, (tm, tm), 1)
    tri = jnp.where((ri // CHUNK == ci // CHUNK) & (ci <= ri), 1.0, 0.0).astype(F32)
    gc = jnp.dot(tri, g, preferred_element_type=F32, precision=lax.Precision.HIGHEST)
    col = jnp.where(lane < DN_HEADS, gc, jax.nn.sigmoid(ab))
    gcol_ref[...] = col
    colt = col.T
    grow_ref[...] = jnp.zeros(grow_ref.shape, F32)
    for c in range(tm // CHUNK):
        grow_ref[:, c * LANES:c * LANES + CHUNK] = colt[0:2 * DN_HEADS, c * CHUNK:(c + 1) * CHUNK]

    for j in range(3):
        pre_ref[DNP_HALO:DNP_HALO + tm, j * W:(j + 1) * W] = jnp.dot(
            hb, wqkvz_ref[:, j * W:(j + 1) * W], preferred_element_type=F32)
    base = DNP_HALO - (SHORT_CONV - 1)
    outs = (q_ref, k_ref, v_ref)
    for j in range(3):
        for h in range(DN_HEADS):
            c0 = j * W + h * DN_HEAD_DIM
            acc = jnp.zeros((tm, DN_HEAD_DIM), F32)
            for k in range(SHORT_CONV):
                acc = acc + wsc_ref[k:k + 1, c0:c0 + DN_HEAD_DIM] * pre_ref[base + k:base + k + tm, c0:c0 + DN_HEAD_DIM]
            y = _silu(acc)
            if j < 2:
                y = y * lax.rsqrt(jnp.sum(y * y, axis=-1, keepdims=True) + EPS)
                if j == 0:
                    y = y * (DN_HEAD_DIM ** -0.5)
            outs[j][:, h * DN_HEAD_DIM:(h + 1) * DN_HEAD_DIM] = y.astype(BF16)
    pre_ref[0:DNP_HALO, :] = pre_ref[tm:tm + DNP_HALO, :]


def _dn_proj(x, mod, layer, norm_g, w_qkvz, w_ab, w_sconv, a_log, dt_bias):
    B, T, D = x.shape
    W = DN_WIDTH
    tok = pl.BlockSpec((None, DNP_TM, W), lambda b, t: (b, t, 0))
    return pl.pallas_call(
        _dnproj_kernel,
        grid=(B, T // DNP_TM),
        in_specs=[
            pl.BlockSpec((None, DNP_TM, D), lambda b, t: (b, t, 0)),
            _mod_spec(layer, 1),
            _const_spec((1, D)),
            _const_spec((D, 4 * W)),
            _const_spec((D, LANES)),
            _const_spec((SHORT_CONV, 3 * W)),
            _const_spec((1, LANES)),
            _const_spec((1, LANES)),
        ],
        out_specs=[tok, tok, tok, tok,
                   pl.BlockSpec((None, DNP_TM, LANES), lambda b, t: (b, t, 0)),
                   pl.BlockSpec((None, 2 * DN_HEADS, DNP_TM // CHUNK * LANES), lambda b, t: (b, 0, t))],
        out_shape=[jax.ShapeDtypeStruct((B, T, W), BF16)] * 4 + [
            jax.ShapeDtypeStruct((B, T, LANES), F32),
            jax.ShapeDtypeStruct((B, 2 * DN_HEADS, T // CHUNK * LANES), F32)],
        scratch_shapes=[pltpu.VMEM((DNP_HALO + DNP_TM, 3 * W), F32)],
        compiler_params=_params("arbitrary", "arbitrary"),
        name="dn_proj",
    )(x, mod, norm_g.reshape(1, D), w_qkvz, w_ab, w_sconv, a_log, dt_bias)


def _dnchunk_kernel(q_ref, k_ref, v_ref, gcol_ref, grow_ref, o_ref, s_ref):
    C = CHUNK
    Dh = DN_HEAD_DIM
    H = DN_HEADS
    NT = (((1,), (1,)), ((), ()))
    TN = (((0,), (0,)), ((), ()))

    @pl.when(pl.program_id(1) == 0)
    def _():
        s_ref[...] = jnp.zeros(s_ref.shape, F32)

    ri = lax.broadcasted_iota(jnp.int32, (C, C), 0)
    ci = lax.broadcasted_iota(jnp.int32, (C, C), 1)
    causal = ci <= ri
    strict = ci < ri
    eye = jnp.where(ri == ci, 1.0, 0.0).astype(F32)
    same = {m: lax.shift_right_logical(ri, s) == lax.shift_right_logical(ci, s)
            for m, s in ((8, 3), (16, 4), (32, 5))}
    off = {8: same[16] & jnp.logical_not(same[8]),
           16: same[32] & jnp.logical_not(same[16]),
           32: jnp.logical_not(same[32])}

    items = [(c, h) for c in range(DNC_CPS) for h in range(H)]
    n = len(items)

    k16, q16, kf, beta, gc, eg, decay, kbeta = ([None] * n for _ in range(8))
    for i, (c, h) in enumerate(items):
        r0 = c * C
        hs = slice(h * Dh, (h + 1) * Dh)
        k16[i] = k_ref[r0:r0 + C, hs]
        q16[i] = q_ref[r0:r0 + C, hs]
        kf[i] = k16[i].astype(F32)
        gc[i] = gcol_ref[r0:r0 + C, h:h + 1]
        beta[i] = gcol_ref[r0:r0 + C, H + h:H + h + 1]
        gr = grow_ref[h:h + 1, c * LANES:c * LANES + C]
        decay[i] = jnp.where(causal, jnp.exp(jnp.where(causal, gc[i] - gr, 0.0)), 0.0)
        kbeta[i] = kf[i] * beta[i]
        eg[i] = jnp.exp(gc[i])

    kq = [lax.dot_general(jnp.concatenate([kbeta[i].astype(BF16), q16[i]], axis=0), k16[i], NT,
                          preferred_element_type=F32) for i in range(n)]
    a_mat = [jnp.where(strict, kq[i][0:C] * decay[i], 0.0) for i in range(n)]
    intra = [(kq[i][C:2 * C] * decay[i]).astype(BF16) for i in range(n)]

    d8 = [jnp.where(same[8], a_mat[i], 0.0) for i in range(n)]
    d8_2 = [_bdot(d8[i], d8[i]) for i in range(n)]
    d8_4 = [_bdot(d8_2[i], d8_2[i]) for i in range(n)]
    p8 = [_bdot(eye - d8[i], eye + d8_2[i]) for i in range(n)]
    tinv = [_bdot(p8[i], eye + d8_4[i]) for i in range(n)]
    for m in (8, 16, 32):
        et = [_bdot(jnp.where(off[m], a_mat[i], 0.0), tinv[i]) for i in range(n)]
        tinv = [tinv[i] - _bdot(tinv[i], et[i]) for i in range(n)]

    uw = []
    for i, (c, h) in enumerate(items):
        r0 = c * C
        vf = v_ref[r0:r0 + C, h * Dh:(h + 1) * Dh].astype(F32)
        rhs = jnp.concatenate([(vf * beta[i]).astype(BF16), (kbeta[i] * eg[i]).astype(BF16)], axis=1)
        uw.append(jnp.dot(tinv[i].astype(BF16), rhs, preferred_element_type=F32))

    state = [s_ref[h] for h in range(H)]
    for c in range(DNC_CPS):
        r0 = c * C
        idx = [c * H + h for h in range(H)]
        sb = [state[h].astype(BF16) for h in range(H)]
        wq = [jnp.concatenate([uw[i][:, Dh:2 * Dh].astype(BF16),
                               (q16[i].astype(F32) * eg[i]).astype(BF16)], axis=0) for i in idx]
        ws = [jnp.dot(wq[h], sb[h], preferred_element_type=F32) for h in range(H)]
        vnb = [(uw[idx[h]][:, 0:Dh] - ws[h][0:C]).astype(BF16) for h in range(H)]
        o2 = [jnp.dot(intra[idx[h]], vnb[h], preferred_element_type=F32) for h in range(H)]
        ds = []
        for h in range(H):
            i = idx[h]
            kd = (kf[i] * jnp.exp(gc[i][C - 1:C, :] - gc[i])).astype(BF16)
            ds.append(lax.dot_general(kd, vnb[h], TN, preferred_element_type=F32))
        for h in range(H):
            i = idx[h]
            o_ref[r0:r0 + C, h * Dh:(h + 1) * Dh] = (ws[h][C:2 * C] + o2[h]).astype(BF16)
            state[h] = state[h] * jnp.exp(gc[i][C - 1:C, :]) + ds[h]
    for h in range(H):
        s_ref[h] = state[h]


def _dn_chunk(q, k, v, gcol, grow):
    B, T, W = q.shape
    tc = DNC_CPS * CHUNK
    tok = pl.BlockSpec((None, tc, W), lambda b, t: (b, t, 0))
    return pl.pallas_call(
        _dnchunk_kernel,
        grid=(B, T // tc),
        in_specs=[tok, tok, tok,
                  pl.BlockSpec((None, tc, LANES), lambda b, t: (b, t, 0)),
                  pl.BlockSpec((None, 2 * DN_HEADS, DNC_CPS * LANES), lambda b, t: (b, 0, t))],
        out_specs=tok,
        out_shape=jax.ShapeDtypeStruct((B, T, W), BF16),
        scratch_shapes=[pltpu.VMEM((DN_HEADS, DN_HEAD_DIM, DN_HEAD_DIM), F32)],
        compiler_params=_params("arbitrary", "arbitrary"),
        name="dn_chunk",
    )(q, k, v, gcol, grow)


def _dnout_ffn_kernel(x_ref, modm_ref, modf_ref, o_ref_in, z_ref, og_ref, wdn_ref, g_ref, win_ref, wout_ref,
                      fg_ref, out_ref, y_ref, a_ref, *, final_norm):
    Dh = DN_HEAD_DIM
    for h in range(DN_HEADS):
        hs = slice(h * Dh, (h + 1) * Dh)
        o = o_ref_in[:, hs].astype(F32)
        ms = jnp.mean(o * o, axis=-1, keepdims=True)
        y = o * lax.rsqrt(ms + EPS) * og_ref[...] * _silu(z_ref[:, hs].astype(F32))
        y_ref[:, hs] = y.astype(BF16)
    mixed = jnp.dot(y_ref[...], wdn_ref[...], preferred_element_type=F32)
    x = x_ref[...] + (1.0 + modm_ref[2:3, :]) * mixed
    out_ref[...] = _ffn_body(lambda rs: x[rs, :], modf_ref, g_ref, win_ref, wout_ref, fg_ref, a_ref, final_norm)


def _dnout_ffn(x, mod, layer, o, z, o_g, w_dn, norm_g, w_in, w_out, final_g, final_norm):
    B, T, D = x.shape
    W = DN_WIDTH
    tok = pl.BlockSpec((None, FFN_TM, W), lambda b, t: (b, t, 0))
    return pl.pallas_call(
        functools.partial(_dnout_ffn_kernel, final_norm=final_norm),
        grid=(B, T // FFN_TM),
        in_specs=[
            pl.BlockSpec((None, FFN_TM, D), lambda b, t: (b, t, 0)),
            _mod_spec(layer, 1),
            _mod_spec(layer, 2),
            tok, tok,
            _const_spec((1, DN_HEAD_DIM)),
            _const_spec((W, D)),
            _const_spec((1, D)),
            _const_spec((D, 2 * D_FF), (layer, 1)),
            _const_spec((D_FF, D), (layer, 1)),
            _const_spec((1, D)),
        ],
        out_specs=pl.BlockSpec((None, FFN_TM, D), lambda b, t: (b, t, 0)),
        out_shape=jax.ShapeDtypeStruct((B, T, D), F32),
        scratch_shapes=[pltpu.VMEM((FFN_TM, W), BF16), pltpu.VMEM((FFN_TM, D_FF), BF16)],
        compiler_params=_params("parallel", "parallel"),
        name="dnout_ffn",
    )(x, mod, mod, o, z, o_g.reshape(1, DN_HEAD_DIM), w_dn, norm_g.reshape(1, D), w_in, w_out,
      final_g.reshape(1, D))


def _lane_pad_row(v):
    return jnp.pad(v.astype(F32), (0, LANES - v.shape[0])).reshape(1, LANES)


def kernel(x, c, norm_g, w_ada, b_ada, w_ffn_in, w_ffn_out, cm_w_glu, cm_b_glu, cm_w_dw, cm_b_dw, cm_ln_g, cm_ln_b, cm_w_pw, cm_b_pw, dn_w_in, dn_w_sconv, dn_a_log, dn_dt_bias, dn_o_g, dn_w_out, final_g):
    B = x.shape[0]
    W = DN_WIDTH
    mod = _ada_mod(c, w_ada, b_ada).reshape(DEPTH, B, N_SUB, 3, D_MODEL)
    w_in, w_out = w_ffn_in.astype(BF16), w_ffn_out.astype(BF16)
    for i in range(DEPTH):
        last = i == DEPTH - 1
        x = _ffn(x, mod, i, 0, norm_g[i, 0], w_in, w_out, final_g, False)
        if i % 2 == 0:
            a = i // 2
            x = _conv_module(x, mod, i, norm_g[i, 1], cm_w_glu[a].astype(BF16), cm_b_glu[a], cm_w_dw[a],
                             cm_b_dw[a], cm_ln_g[a], cm_ln_b[a], cm_w_pw[a].astype(BF16), cm_b_pw[a])
            x = _ffn(x, mod, i, 1, norm_g[i, 2], w_in, w_out, final_g, last)
        else:
            m = i // 2
            w_dn = dn_w_in[m]
            w_ab = jnp.pad(w_dn[:, 4 * W:], ((0, 0), (0, LANES - 2 * DN_HEADS))).astype(BF16)
            q, k, v, z, gcol, grow = _dn_proj(
                x, mod, i, norm_g[i, 1], w_dn[:, :4 * W].astype(BF16), w_ab, dn_w_sconv[m],
                _lane_pad_row(dn_a_log[m]), _lane_pad_row(dn_dt_bias[m]))
            o = _dn_chunk(q, k, v, gcol, grow)
            x = _dnout_ffn(x, mod, i, o, z, dn_o_g[m], dn_w_out[m].astype(BF16), norm_g[i, 2], w_in, w_out,
                           final_g, last)
    return x
```
